```python
import functools
import jax, jax.numpy as jnp
from jax import lax
import numpy as np

D_MODEL = 1024
BATCH = 8
SEQ = 2048
DEPTH = 4
DEC_BATCH = 128
DEC_SEQ = 4
PAST_LEN = 2048
PAGE_SIZE = 128

N_A_LAYERS = DEPTH // 2
N_B_LAYERS = DEPTH - N_A_LAYERS
GLA_HEADS = 4
GLA_QK = D_MODEL // 2
GLA_V = D_MODEL
GLA_DK = GLA_QK // GLA_HEADS
GLA_DV = GLA_V // GLA_HEADS
GLA_LR = 16
GLA_TAU = 16.0
GLA_CHUNK = 64
GLA_IN = 2 * GLA_QK + 2 * GLA_V + GLA_LR
HEAD_DIM = 128
N_KV = 4
Q_PER_KV = 2
N_QH_GROUP = N_KV * Q_PER_KV
DIL_GROUPS = ((128, 1), (512, 4), (2048, 16))
N_GROUPS = len(DIL_GROUPS)
MAX_WINDOW = max(w for w, _ in DIL_GROUPS)
ROPE_DIM = HEAD_DIM // 4
ROPE_THETA = 500000.0
FFN_HIDDEN = ((8 * D_MODEL + 3 * 256 - 1) // (3 * 256)) * 256
PLE_DIM = 256
EPS = 1e-6
NEG_INF = -1e30

kernel_name = 'yoco_gla_dilated_swa_decode_step'


def rms_norm(x, g):
    xf = x.astype(jnp.float32)
    y = xf * lax.rsqrt(jnp.mean(xf * xf, axis=-1, keepdims=True) + EPS)
    return (y * g.astype(jnp.float32)).astype(x.dtype)


def rope_partial(x, pos):
    half = ROPE_DIM // 2
    freqs = jnp.power(ROPE_THETA, -jnp.arange(half, dtype=jnp.float32) * 2.0 / ROPE_DIM)
    ang = pos[:, None] * freqs[None, :]
    shape = (ang.shape[0],) + (1,) * (x.ndim - 3) + (half,)
    cos = jnp.cos(ang).reshape(shape)
    sin = jnp.sin(ang).reshape(shape)
    xf = x.astype(jnp.float32)
    x1 = xf[..., :half]
    x2 = xf[..., half:ROPE_DIM]
    out = jnp.concatenate([x1 * cos - x2 * sin, x2 * cos + x1 * sin, xf[..., ROPE_DIM:]], axis=-1)
    return out.astype(x.dtype)


def gla_chunked(q, k, v, log_a, s0):
    B, L, H, DK = q.shape
    DV = v.shape[-1]
    C = GLA_CHUNK if L % GLA_CHUNK == 0 else L
    n = L // C

    def blocks(t):
        return t.reshape(B, n, C, H, t.shape[-1]).transpose(1, 0, 3, 2, 4)

    qb, kb, vb, gb = blocks(q), blocks(k), blocks(v), blocks(log_a)
    cum = jnp.cumsum(gb, axis=3)
    last = cum[:, :, :, -1:, :]
    q_in = qb * jnp.exp(cum)
    k_in = kb * jnp.exp(-cum)
    k_end = kb * jnp.exp(last - cum)
    causal = jnp.tril(jnp.ones((C, C), dtype=bool))
    att = jnp.where(causal, jnp.einsum('nbhcd,nbhsd->nbhcs', q_in, k_in), 0.0)
    o_intra = jnp.einsum('nbhcs,nbhsv->nbhcv', att, vb)
    decay = jnp.exp(last[:, :, :, 0, :])

    def step(s, inp):
        qi, ke, vi, dc = inp
        o = jnp.einsum('bhcd,bhdv->bhcv', qi, s)
        s = dc[..., None] * s + jnp.einsum('bhcd,bhcv->bhdv', ke, vi)
        return s, o

    s_fin, o_inter = lax.scan(step, s0, (q_in, k_end, vb, decay))
    o = (o_intra + o_inter).transpose(1, 0, 3, 2, 4).reshape(B, L, H, DV)
    return o, s_fin


def gla_mixer(h, w_in, w_gk_up, b_gk, onorm, w_o, s0):
    B, L, _ = h.shape
    proj = h @ w_in
    q, k, v, r, glr = jnp.split(proj, [GLA_QK, 2 * GLA_QK, 2 * GLA_QK + GLA_V, 2 * GLA_QK + 2 * GLA_V], axis=-1)
    log_a = jax.nn.log_sigmoid((glr @ w_gk_up + b_gk).astype(jnp.float32)) / GLA_TAU

    def heads(t, d):
        return t.astype(jnp.float32).reshape(B, L, GLA_HEADS, d)

    o, s_new = gla_chunked(heads(q, GLA_DK) * (GLA_DK ** -0.5), heads(k, GLA_DK), heads(v, GLA_DV),
                           log_a.reshape(B, L, GLA_HEADS, GLA_DK), s0.astype(jnp.float32))
    o = rms_norm(o, onorm) * jax.nn.silu(r.astype(jnp.float32)).reshape(B, L, GLA_HEADS, GLA_DV)
    return o.reshape(B, L, GLA_V).astype(h.dtype) @ w_o, s_new


def dilated_band_attention(q, win, dil, k, v):
    B, L = q.shape[:2]
    n = win // dil
    M = L // dil
    nb = -(-M // n)
    Mp = nb * n

    def by_class(t):
        return jnp.moveaxis(t.reshape((B, M, dil) + t.shape[2:]), 2, 1)

    qc = jnp.pad(by_class(q), ((0, 0), (0, 0), (0, Mp - M), (0, 0), (0, 0), (0, 0)))
    qb = qc.reshape(B, dil, nb, n, N_KV, Q_PER_KV, HEAD_DIM)

    def band_keys(t):
        t = jnp.pad(by_class(t), ((0, 0), (0, 0), (n, Mp - M), (0, 0), (0, 0))).reshape(B, dil, nb + 1, n, N_KV, HEAD_DIM)
        return jnp.concatenate([t[:, :, :-1], t[:, :, 1:]], axis=3)

    kb, vb = band_keys(k), band_keys(v)
    s = jnp.einsum('bcnikgd,bcnjkd->bcnkgij', qb, kb, preferred_element_type=jnp.float32) * (HEAD_DIM ** -0.5)
    qi = jnp.arange(nb)[:, None, None] * n + jnp.arange(n)[None, :, None]
    kj = jnp.arange(nb)[:, None, None] * n - n + jnp.arange(2 * n)[None, None, :]
    valid = (kj <= qi) & (kj >= qi - n) & (kj >= 0)
    s = jnp.where(valid[:, None, None], s, NEG_INF)
    lse = jax.nn.logsumexp(s, axis=-1)
    p = jnp.exp(s - lse[..., None])
    o = jnp.einsum('bcnkgij,bcnjkd->bcnikgd', p, vb, preferred_element_type=jnp.float32)
    o = o.reshape(B, dil, Mp, N_KV, Q_PER_KV, HEAD_DIM)[:, :, :M]
    o = jnp.moveaxis(o, 1, 2).reshape(B, L, N_KV, Q_PER_KV, HEAD_DIM)
    lse = jnp.moveaxis(lse, -1, 3).reshape(B, dil, Mp, N_KV, Q_PER_KV)[:, :, :M]
    lse = jnp.moveaxis(lse, 1, 2).reshape(B, L, N_KV, Q_PER_KV)
    return o, lse


def dilated_gather_attention(q, win, dil, k_all, v_all):
    T = q.shape[1]
    S = k_all.shape[1]
    n = win // dil
    idx = (S - T) + jnp.arange(T)[:, None] - dil * jnp.arange(n + 1)[None, :]
    valid = idx >= 0
    idx = jnp.maximum(idx, 0)
    kg = jnp.take(k_all, idx, axis=1)
    vg = jnp.take(v_all, idx, axis=1)
    s = jnp.einsum('btkgd,btikd->btkgi', q, kg, preferred_element_type=jnp.float32) * (HEAD_DIM ** -0.5)
    s = jnp.where(valid[:, None, None, :], s, NEG_INF)
    lse = jax.nn.logsumexp(s, axis=-1)
    p = jnp.exp(s - lse[..., None])
    o = jnp.einsum('btkgi,btikd->btkgd', p, vg, preferred_element_type=jnp.float32)
    return o, lse


def dilated_mixer(h, w_q, q_norm, w_o, pos, attend):
    B, L, _ = h.shape
    q = (h @ w_q).reshape(B, L, N_GROUPS, N_KV, Q_PER_KV, HEAD_DIM)
    q = rope_partial(rms_norm(q, q_norm), pos)
    outs, lses = [], []
    for g, (win, dil) in enumerate(DIL_GROUPS):
        o, l = attend(q[:, :, g], win, dil)
        outs.append(o)
        lses.append(l)
    w = jax.nn.softmax(jnp.stack(lses), axis=0)
    o = jnp.sum(w[..., None] * jnp.stack(outs), axis=0)
    return o.reshape(B, L, N_QH_GROUP * HEAD_DIM).astype(h.dtype) @ w_o


def shared_kv(x, kv_norm, w_kv, k_norm, pos):
    B, L, _ = x.shape
    kv = (rms_norm(x, kv_norm) @ w_kv).reshape(B, L, 2, N_KV, HEAD_DIM)
    k = rope_partial(rms_norm(kv[:, :, 0], k_norm), pos)
    return k, kv[:, :, 1]


def swiglu_ffn(x, g, w_in, w_out):
    a, b = jnp.split(rms_norm(x, g) @ w_in, 2, axis=-1)
    return (jax.nn.silu(a) * b) @ w_out


def per_layer_embed(x, p, g, w_proj, w_gate):
    return (p @ w_proj) * jax.nn.sigmoid(rms_norm(x, g) @ w_gate)


def run_trunk(x, p, gla_s0, k_past, v_past, pos, a_norm, a_w_in, a_w_gk_up, a_b_gk, a_onorm, a_w_o,
              kv_norm, w_kv, k_norm, b_norm, b_w_q, b_q_norm, b_w_o, f_norm, f_w_in, f_w_out,
              e_norm, e_w_proj, e_w_gate):
    gla_states = []
    k_new = v_new = attend = None
    for i in range(DEPTH):
        if i < N_A_LAYERS:
            h, s = gla_mixer(rms_norm(x, a_norm[i]), a_w_in[i], a_w_gk_up[i], a_b_gk[i], a_onorm[i], a_w_o[i], gla_s0[i])
            gla_states.append(s)
        else:
            j = i - N_A_LAYERS
            if j == 0:
                k_new, v_new = shared_kv(x, kv_norm, w_kv, k_norm, pos)
                if k_past is None:
                    attend = functools.partial(dilated_band_attention, k=k_new, v=v_new)
                else:
                    k_all = jnp.concatenate([k_past.astype(k_new.dtype), k_new], axis=1)
                    v_all = jnp.concatenate([v_past.astype(v_new.dtype), v_new], axis=1)
                    attend = functools.partial(dilated_gather_attention, k_all=k_all, v_all=v_all)
            h = dilated_mixer(rms_norm(x, b_norm[j]), b_w_q[j], b_q_norm[j], b_w_o[j], pos, attend)
        x = x + h.astype(x.dtype)
        x = x + swiglu_ffn(x, f_norm[i], f_w_in[i], f_w_out[i]).astype(x.dtype)
        x = x + per_layer_embed(x, p[i], e_norm[i], e_w_proj[i], e_w_gate[i]).astype(x.dtype)
    return x, jnp.stack(gla_states), k_new, v_new


def setup_inputs(seed: int = 0) -> dict:
    key = jax.random.key(seed)
    keys = iter(jax.random.split(key, 32))

    def nrm(shape, scale):
        return jax.random.normal(next(keys), shape, jnp.float32) * scale

    def gain(shape):
        return 1.0 + nrm(shape, 0.02)

    buf = min(MAX_WINDOW, PAST_LEN)
    return {
        'x_prompt': nrm((BATCH, SEQ, D_MODEL), 1.0),
        'x_sample': nrm((DEC_BATCH, DEC_SEQ, D_MODEL), 1.0),
        'p_prompt': nrm((DEPTH, BATCH, SEQ, PLE_DIM), 1.0),
        'p_sample': nrm((DEPTH, DEC_BATCH, DEC_SEQ, PLE_DIM), 1.0),
        'state_gla': nrm((N_A_LAYERS, DEC_BATCH, GLA_HEADS, GLA_DK, GLA_DV), 0.5),
        'cache_k': nrm((DEC_BATCH, buf, N_KV, HEAD_DIM), 1.0),
        'cache_v': nrm((DEC_BATCH, buf, N_KV, HEAD_DIM), 1.0),
        'a_norm': gain((N_A_LAYERS, D_MODEL)),
        'a_w_in': nrm((N_A_LAYERS, D_MODEL, GLA_IN), D_MODEL ** -0.5),
        'a_w_gk_up': nrm((N_A_LAYERS, GLA_LR, GLA_QK), GLA_LR ** -0.5),
        'a_b_gk': nrm((N_A_LAYERS, GLA_QK), 0.1),
        'a_onorm': gain((N_A_LAYERS, GLA_DV)),
        'a_w_o': nrm((N_A_LAYERS, GLA_V, D_MODEL), GLA_V ** -0.5),
        'kv_norm': gain((D_MODEL,)),
        'w_kv': nrm((D_MODEL, 2 * N_KV * HEAD_DIM), D_MODEL ** -0.5),
        'k_norm': gain((HEAD_DIM,)),
        'b_norm': gain((N_B_LAYERS, D_MODEL)),
        'b_w_q': nrm((N_B_LAYERS, D_MODEL, N_GROUPS * N_QH_GROUP * HEAD_DIM), D_MODEL ** -0.5),
        'b_q_norm': gain((N_B_LAYERS, HEAD_DIM)),
        'b_w_o': nrm((N_B_LAYERS, N_QH_GROUP * HEAD_DIM, D_MODEL), (N_QH_GROUP * HEAD_DIM) ** -0.5),
        'f_norm': gain((DEPTH, D_MODEL)),
        'f_w_in': nrm((DEPTH, D_MODEL, 2 * FFN_HIDDEN), D_MODEL ** -0.5),
        'f_w_out': nrm((DEPTH, FFN_HIDDEN, D_MODEL), FFN_HIDDEN ** -0.5),
        'e_norm': gain((DEPTH, D_MODEL)),
        'e_w_proj': nrm((DEPTH, PLE_DIM, D_MODEL), PLE_DIM ** -0.5),
        'e_w_gate': nrm((DEPTH, D_MODEL, D_MODEL), D_MODEL ** -0.5),
    }


def reference(x_prompt, x_sample, p_prompt, p_sample, state_gla, cache_k, cache_v,
              a_norm, a_w_in, a_w_gk_up, a_b_gk, a_onorm, a_w_o, kv_norm, w_kv, k_norm,
              b_norm, b_w_q, b_q_norm, b_w_o, f_norm, f_w_in, f_w_out, e_norm, e_w_proj, e_w_gate):
    weights = (a_norm, a_w_in, a_w_gk_up, a_b_gk, a_onorm, a_w_o, kv_norm, w_kv, k_norm,
               b_norm, b_w_q, b_q_norm, b_w_o, f_norm, f_w_in, f_w_out, e_norm, e_w_proj, e_w_gate)
    seq = x_prompt.shape[1]
    dec_seq = x_sample.shape[1]
    pos_prompt = jnp.arange(seq, dtype=jnp.float32)
    pos_sample = PAST_LEN + jnp.arange(dec_seq, dtype=jnp.float32)
    s0 = jnp.zeros((N_A_LAYERS, x_prompt.shape[0], GLA_HEADS, GLA_DK, GLA_DV), jnp.float32)
    y_prompt, gla_state_prompt, k_p, v_p = run_trunk(x_prompt, p_prompt, s0, None, None, pos_prompt, *weights)
    y_sample, gla_state_sample, k_sample, v_sample = run_trunk(x_sample, p_sample, state_gla, cache_k, cache_v, pos_sample, *weights)
    keep = min(MAX_WINDOW, seq)
    k_prompt = k_p[:, seq - keep:]
    v_prompt = v_p[:, seq - keep:]
    return (y_prompt, y_sample, gla_state_prompt, gla_state_sample, k_prompt, v_prompt, k_sample, v_sample)
```

```python
import functools
import math

import jax
import jax.numpy as jnp
from jax import lax
from jax.experimental import pallas as pl
from jax.experimental.pallas import tpu as pltpu

D_MODEL = 1024
PAST_LEN = 2048
GLA_HEADS = 4
GLA_DK = 128
GLA_DV = 256
GLA_QK = GLA_HEADS * GLA_DK
GLA_V = GLA_HEADS * GLA_DV
GLA_LR = 16
GLA_TAU = 16.0
GLA_CHUNK = 64
HEAD_DIM = 128
N_KV = 4
Q_PER_KV = 2
N_QH_GROUP = N_KV * Q_PER_KV
DIL_GROUPS = ((128, 1), (512, 4), (2048, 16))
N_GROUPS = len(DIL_GROUPS)
MAX_WINDOW = 2048
ROPE_DIM = HEAD_DIM // 4
ROPE_HALF = ROPE_DIM // 2
ROPE_THETA = 500000.0
FFN_HIDDEN = 2816
PLE_DIM = 256
EPS = 1e-6
NEG_INF = -1e30

LANES = 128
GLR_PAD = LANES
VMEM_LIMIT = 56 * 1024 * 1024

F32 = jnp.float32
BF16 = jnp.bfloat16


def _params(*sem):
    return pltpu.CompilerParams(dimension_semantics=sem, vmem_limit_bytes=VMEM_LIMIT)


def _resident(shape):
    zeros = (0,) * len(shape)
    return pl.BlockSpec(shape, lambda *_: zeros, pipeline_mode=pl.Buffered(1))


def _rms(x, g):
    var = jnp.mean(x * x, axis=-1, keepdims=True)
    return x * lax.rsqrt(var + EPS) * g


def _log_sigmoid(x):
    return jnp.minimum(x, 0.0) - jnp.log1p(jnp.exp(-jnp.abs(x)))


def _dot(a, b):
    return jnp.dot(a, b, preferred_element_type=F32)


def _dot_nt(a, b):
    return lax.dot_general(a, b, (((1,), (1,)), ((), ())), preferred_element_type=F32)


def _dot_tn(a, b):
    return lax.dot_general(a, b, (((0,), (0,)), ((), ())), preferred_element_type=F32)


def _proj_a_kernel(x_ref, g_ref, w_ref, qk_ref, v_ref, r_ref, glr_ref):
    h = _rms(x_ref[...], g_ref[...]).astype(BF16)
    qk_ref[...] = _dot(h, w_ref[:, 0:2 * GLA_QK])
    v_ref[...] = _dot(h, w_ref[:, 2 * GLA_QK:2 * GLA_QK + GLA_V]).astype(v_ref.dtype)
    r_ref[...] = _dot(h, w_ref[:, 2 * GLA_QK + GLA_V:2 * GLA_QK + 2 * GLA_V])
    glr_ref[...] = _dot(h, w_ref[:, 2 * GLA_QK + 2 * GLA_V:]).astype(BF16)


def _proj_a(x, g, w, v_dtype, tm):
    m = x.shape[0]
    n = w.shape[1]
    row = lambda width: pl.BlockSpec((tm, width), lambda i: (i, 0))
    return pl.pallas_call(
        _proj_a_kernel,
        grid=(m // tm,),
        in_specs=[row(D_MODEL), _resident((1, D_MODEL)), _resident((D_MODEL, n))],
        out_specs=[row(2 * GLA_QK), row(GLA_V), row(GLA_V), row(GLR_PAD)],
        out_shape=[jax.ShapeDtypeStruct((m, 2 * GLA_QK), F32),
                   jax.ShapeDtypeStruct((m, GLA_V), v_dtype),
                   jax.ShapeDtypeStruct((m, GLA_V), F32),
                   jax.ShapeDtypeStruct((m, GLR_PAD), BF16)],
        compiler_params=_params("parallel"),
        name="proj_a",
    )(x, g, w)


def _split3(x):
    hi = x.astype(BF16)
    r1 = x - hi.astype(F32)
    mid = r1.astype(BF16)
    lo = (r1 - mid.astype(F32)).astype(BF16)
    return hi, mid, lo


def _gla_seq_kernel(qk_ref, v_ref, r_ref, glr_ref, wgk_ref, bgk_ref, on_ref, o_ref, s_ref, *, T, C):
    @pl.when(pl.program_id(1) == 0)
    def _():
        s_ref[...] = jnp.zeros_like(s_ref)

    logits = _dot(glr_ref[0], wgk_ref[...]) + bgk_ref[...]
    log_a = _log_sigmoid(logits) * (1.0 / GLA_TAU)
    row = lax.broadcasted_iota(jnp.int32, (C, C), 0)
    col = lax.broadcasted_iota(jnp.int32, (C, C), 1)
    causal = col <= row
    tril = jnp.where(causal, 1.0, 0.0).astype(BF16)
    onorm = on_ref[...]
    for h in range(GLA_HEADS):
        state = s_ref[0, h]
        for c in range(T // C):
            rows = slice(c * C, (c + 1) * C)
            la = log_a[rows, h * GLA_DK:(h + 1) * GLA_DK]
            hi, mid, lo = _split3(la)
            cum3 = _dot(tril, jnp.concatenate([hi, mid, lo], axis=1))
            cum = cum3[:, :GLA_DK] + cum3[:, GLA_DK:2 * GLA_DK] + cum3[:, 2 * GLA_DK:]
            last = cum[C - 1:C, :]
            q = qk_ref[0, rows, h * GLA_DK:(h + 1) * GLA_DK] * (GLA_DK ** -0.5)
            k = qk_ref[0, rows, GLA_QK + h * GLA_DK:GLA_QK + (h + 1) * GLA_DK]
            v = v_ref[0, rows, h * GLA_DV:(h + 1) * GLA_DV]
            q_in = (q * jnp.exp(cum)).astype(BF16)
            k_in = (k * jnp.exp(-cum)).astype(BF16)
            k_end = (k * jnp.exp(last - cum)).astype(BF16)
            att = jnp.where(causal, _dot_nt(q_in, k_in), 0.0).astype(BF16)
            o = _dot(att, v) + _dot(q_in, state.astype(BF16))
            decay = jnp.exp(jnp.broadcast_to(last, (GLA_DK, GLA_DK)).T)
            state = state * jnp.concatenate([decay, decay], axis=1) + _dot_tn(k_end, v)
            gate = r_ref[0, rows, h * GLA_DV:(h + 1) * GLA_DV]
            og = _rms(o, onorm) * (gate * jax.nn.sigmoid(gate))
            o_ref[0, rows, h * GLA_DV:(h + 1) * GLA_DV] = og.astype(o_ref.dtype)
        s_ref[0, h] = state


def _gla_seq(qk, v, r, glr, wgk, bgk, onorm, T):
    b, l, _ = qk.shape
    tok = lambda width: pl.BlockSpec((1, T, width), lambda i, t: (i, t, 0))
    return pl.pallas_call(
        functools.partial(_gla_seq_kernel, T=T, C=GLA_CHUNK),
        grid=(b, l // T),
        in_specs=[tok(2 * GLA_QK), tok(GLA_V), tok(GLA_V), tok(GLR_PAD),
                  _resident((GLR_PAD, GLA_QK)), _resident((1, GLA_QK)), _resident((1, GLA_DV))],
        out_specs=[tok(GLA_V),
                   pl.BlockSpec((1, GLA_HEADS, GLA_DK, GLA_DV), lambda i, t: (i, 0, 0, 0))],
        out_shape=[jax.ShapeDtypeStruct((b, l, GLA_V), BF16),
                   jax.ShapeDtypeStruct((b, GLA_HEADS, GLA_DK, GLA_DV), F32)],
        compiler_params=_params("parallel", "arbitrary"),
        name="gla_seq",
    )(qk, v, r, glr, wgk, bgk, onorm)


def _columns(x):
    pad = jnp.zeros((LANES - x.shape[0], LANES), F32)
    return jnp.concatenate([x, pad], axis=0).T


def _gla_step_kernel(qk_ref, v_ref, r_ref, glr_ref, wgk_ref, bgk_ref, on_ref, s0_ref, o_ref, s_ref, *, nb, L):
    logits = _dot(glr_ref[...], wgk_ref[...]) + bgk_ref[...]
    a = jnp.exp(_log_sigmoid(logits) * (1.0 / GLA_TAU))
    onorm = on_ref[...]
    for h in range(GLA_HEADS):
        lanes = slice(h * GLA_DK, (h + 1) * GLA_DK)
        q_cols = _columns(qk_ref[:, lanes] * (GLA_DK ** -0.5))
        k_cols = _columns(qk_ref[:, GLA_QK + h * GLA_DK:GLA_QK + (h + 1) * GLA_DK])
        a_cols = _columns(a[:, lanes])
        outs = []
        for j in range(nb):
            state = s0_ref[j, h]
            for t in range(L):
                i = j * L + t
                v_row = v_ref[i:i + 1, h * GLA_DV:(h + 1) * GLA_DV]
                state = a_cols[:, i:i + 1] * state + k_cols[:, i:i + 1] * v_row
                outs.append(jnp.sum(q_cols[:, i:i + 1] * state, axis=0, keepdims=True))
            s_ref[j, h] = state
        o = jnp.concatenate(outs, axis=0)
        gate = r_ref[:, h * GLA_DV:(h + 1) * GLA_DV]
        og = _rms(o, onorm) * (gate * jax.nn.sigmoid(gate))
        o_ref[:, h * GLA_DV:(h + 1) * GLA_DV] = og.astype(o_ref.dtype)


def _gla_step(qk, v, r, glr, wgk, bgk, onorm, s0, L, nb):
    m = qk.shape[0]
    b = m // L
    rows = nb * L
    tok = lambda width: pl.BlockSpec((rows, width), lambda i: (i, 0))
    st = pl.BlockSpec((nb, GLA_HEADS, GLA_DK, GLA_DV), lambda i: (i, 0, 0, 0))
    return pl.pallas_call(
        functools.partial(_gla_step_kernel, nb=nb, L=L),
        grid=(b // nb,),
        in_specs=[tok(2 * GLA_QK), tok(GLA_V), tok(GLA_V), tok(GLR_PAD),
                  _resident((GLR_PAD, GLA_QK)), _resident((1, GLA_QK)), _resident((1, GLA_DV)), st],
        out_specs=[tok(GLA_V), st],
        out_shape=[jax.ShapeDtypeStruct((m, GLA_V), BF16),
                   jax.ShapeDtypeStruct((b, GLA_HEADS, GLA_DK, GLA_DV), F32)],
        compiler_params=_params("parallel"),
        name="gla_step",
    )(qk, v, r, glr, wgk, bgk, onorm, s0)


def _head_proj_kernel(x_ref, g_ref, w_ref, hg_ref, rope_ref, *out_refs, n_rot, splits):
    h = _rms(x_ref[...], g_ref[...]).astype(BF16)
    y = _dot(h, w_ref[...])
    hg = hg_ref[...]
    cos, sin_lo, sin_hi = rope_ref[0], rope_ref[1], rope_ref[2]
    heads = []
    for i in range(y.shape[1] // HEAD_DIM):
        yh = y[:, i * HEAD_DIM:(i + 1) * HEAD_DIM]
        if i < n_rot:
            yn = _rms(yh, hg)
            yh = (yn * cos + pltpu.roll(yn, HEAD_DIM - ROPE_HALF, axis=1) * sin_lo
                  + pltpu.roll(yn, ROPE_HALF, axis=1) * sin_hi)
        heads.append(yh)
    start = 0
    for ref, width in zip(out_refs, splits):
        n = width // HEAD_DIM
        ref[...] = jnp.concatenate(heads[start:start + n], axis=1).astype(ref.dtype)
        start += n


def _head_proj(x, g, w, hg, rope, n_rot, splits, dtypes, tm):
    m = x.shape[0]
    n = w.shape[1]
    n_rope_tiles = rope.shape[1] // tm
    row = lambda width: pl.BlockSpec((tm, width), lambda i: (i, 0))
    return pl.pallas_call(
        functools.partial(_head_proj_kernel, n_rot=n_rot, splits=splits),
        grid=(m // tm,),
        in_specs=[row(D_MODEL), _resident((1, D_MODEL)), _resident((D_MODEL, n)), _resident((1, HEAD_DIM)),
                  pl.BlockSpec((3, tm, HEAD_DIM), lambda i: (0, i % n_rope_tiles, 0))],
        out_specs=[row(wd) for wd in splits],
        out_shape=[jax.ShapeDtypeStruct((m, wd), dt) for wd, dt in zip(splits, dtypes)],
        compiler_params=_params("parallel"),
        name="head_proj",
    )(x, g, w, hg, rope)


def _rope_tables(pos):
    freqs = jnp.power(ROPE_THETA, -jnp.arange(ROPE_HALF, dtype=F32) * 2.0 / ROPE_DIM)
    ang = pos[:, None] * freqs[None, :]
    cos, sin = jnp.cos(ang), jnp.sin(ang)
    n = pos.shape[0]
    rest = HEAD_DIM - ROPE_DIM
    c = jnp.concatenate([cos, cos, jnp.ones((n, rest), F32)], axis=1)
    s_lo = jnp.concatenate([-sin, jnp.zeros((n, ROPE_HALF + rest), F32)], axis=1)
    s_hi = jnp.concatenate([jnp.zeros((n, ROPE_HALF), F32), sin, jnp.zeros((n, rest), F32)], axis=1)
    return jnp.stack([c, s_lo, s_hi])


BAND = 128


def _attn_seq_kernel(*refs, L):
    n_q = N_GROUPS * Q_PER_KV
    q_refs = refs[:n_q]
    k_ref, v_ref, o_ref, acc_ref, m_ref, l_ref = refs[n_q:]
    scale = HEAD_DIM ** -0.5
    n2 = Q_PER_KV * BAND
    qi = lax.broadcasted_iota(jnp.int32, (n2, 2 * BAND), 0) % BAND
    kj = lax.broadcasted_iota(jnp.int32, (n2, 2 * BAND), 1)
    band_mask = (kj >= qi) & (kj <= qi + BAND)
    first_mask = (lax.broadcasted_iota(jnp.int32, (n2, BAND), 1)
                  <= lax.broadcasted_iota(jnp.int32, (n2, BAND), 0) % BAND)

    for g, (win, dil) in enumerate(DIL_GROUPS):
        assert win // dil == BAND
        for r in range(dil):
            for blk in range(L // dil // BAND):
                q_rows = pl.ds(r + dil * BAND * blk, BAND, stride=dil)
                q = jnp.concatenate([q_refs[g * Q_PER_KV + j][0, q_rows, :]
                                     for j in range(Q_PER_KV)], axis=0).astype(BF16)
                if blk == 0:
                    k_rows, mask = q_rows, first_mask
                else:
                    k_rows = pl.ds(r + dil * BAND * (blk - 1), 2 * BAND, stride=dil)
                    mask = band_mask
                kb = k_ref[0, k_rows, :].astype(BF16)
                vb = v_ref[0, k_rows, :].astype(BF16)
                s = jnp.where(mask, _dot_nt(q, kb) * scale, NEG_INF)
                for j in range(Q_PER_KV):
                    s_j = s[j * BAND:(j + 1) * BAND]
                    m_blk = jnp.max(s_j, axis=-1, keepdims=True)
                    if g == 0:
                        m_new = jnp.broadcast_to(m_blk, (BAND, LANES))
                    else:
                        m_old = m_ref[j, q_rows, :]
                        m_new = jnp.maximum(m_old, m_blk)
                    p = jnp.exp(s_j - m_new[:, :1])
                    psum = jnp.sum(p, axis=-1, keepdims=True)
                    pv = _dot(p.astype(BF16), vb)
                    if g == 0:
                        l_new = jnp.broadcast_to(psum, (BAND, LANES))
                        acc_new = pv
                    else:
                        alpha = jnp.exp(m_old - m_new)
                        l_new = alpha * l_ref[j, q_rows, :] + psum
                        acc_new = alpha * acc_ref[j, q_rows, :] + pv
                    m_ref[j, q_rows, :] = m_new
                    l_ref[j, q_rows, :] = l_new
                    acc_ref[j, q_rows, :] = acc_new
    for j in range(Q_PER_KV):
        o_ref[0, :, j * HEAD_DIM:(j + 1) * HEAD_DIM] = (acc_ref[j] / l_ref[j]).astype(o_ref.dtype)


def _attn_seq(q, k, v):
    b, l, _ = q.shape
    pair = Q_PER_KV * HEAD_DIM
    q_spec = lambda g, j: pl.BlockSpec((1, l, HEAD_DIM), lambda i, h: (i, 0, (g * N_KV + h) * Q_PER_KV + j))
    q_specs = [q_spec(g, j) for g in range(N_GROUPS) for j in range(Q_PER_KV)]
    kv_spec = pl.BlockSpec((1, l, HEAD_DIM), lambda i, h: (i, 0, h))
    stat = pltpu.VMEM((Q_PER_KV, l, LANES), F32)
    return pl.pallas_call(
        functools.partial(_attn_seq_kernel, L=l),
        grid=(b, N_KV),
        in_specs=q_specs + [kv_spec, kv_spec],
        out_specs=pl.BlockSpec((1, l, pair), lambda i, h: (i, 0, h)),
        out_shape=jax.ShapeDtypeStruct((b, l, N_QH_GROUP * HEAD_DIM), BF16),
        scratch_shapes=[stat, stat, stat],
        compiler_params=_params("parallel", "arbitrary"),
        name="attn_seq",
    )(*([q] * len(q_specs)), k, v)


def _attn_step_kernel(q_ref, kn_ref, vn_ref, k0_ref, v0_ref, k1_ref, v1_ref, k2_ref, v2_ref, o_ref, *, T):
    scale = HEAD_DIM ** -0.5
    tile_refs = ((k0_ref, v0_ref), (k1_ref, v1_ref), (k2_ref, v2_ref))
    key_idx = lax.broadcasted_iota(jnp.int32, (BAND, 1), 0)
    new_idx = lax.broadcasted_iota(jnp.int32, (T, 1), 0)
    kv_w = N_KV * HEAD_DIM
    for h in range(N_KV):
        lanes = slice(h * HEAD_DIM, (h + 1) * HEAD_DIM)
        k_new = kn_ref[0, :, lanes]
        v_new = vn_ref[0, :, lanes]
        for t in range(T):
            stats = [[] for _ in range(Q_PER_KV)]
            for g, (win, dil) in enumerate(DIL_GROUPS):
                k_ref, v_ref = tile_refs[g]
                if dil == 1:
                    k_tile, v_tile = k_ref[0, :, lanes], v_ref[0, :, lanes]
                    tile_ok = key_idx >= t
                    new_ok = new_idx <= t
                else:
                    off = t * kv_w + h * HEAD_DIM
                    k_tile, v_tile = k_ref[0, :, off:off + HEAD_DIM], v_ref[0, :, off:off + HEAD_DIM]
                    tile_ok = None
                    new_ok = new_idx == t
                for j in range(Q_PER_KV):
                    col = ((g * N_KV + h) * Q_PER_KV + j) * HEAD_DIM
                    q = q_ref[0, t:t + 1, col:col + HEAD_DIM]
                    s_tile = jnp.sum(k_tile * q, axis=-1, keepdims=True) * scale
                    if tile_ok is not None:
                        s_tile = jnp.where(tile_ok, s_tile, NEG_INF)
                    s_new = jnp.where(new_ok, jnp.sum(k_new * q, axis=-1, keepdims=True) * scale, NEG_INF)
                    m = jnp.maximum(jnp.max(s_tile, axis=0, keepdims=True), jnp.max(s_new, axis=0, keepdims=True))
                    p_tile = jnp.exp(s_tile - m)
                    p_new = jnp.exp(s_new - m)
                    l = jnp.sum(p_tile, axis=0, keepdims=True) + jnp.sum(p_new, axis=0, keepdims=True)
                    acc = (jnp.sum(p_tile * v_tile, axis=0, keepdims=True)
                           + jnp.sum(p_new * v_new, axis=0, keepdims=True))
                    stats[j].append((m, l, acc))
            for j in range(Q_PER_KV):
                m_all = functools.reduce(jnp.maximum, [m for m, _, _ in stats[j]])
                l_all = sum(l * jnp.exp(m - m_all) for m, l, _ in stats[j])
                acc_all = sum(acc * jnp.exp(m - m_all) for m, _, acc in stats[j])
                col = (h * Q_PER_KV + j) * HEAD_DIM
                o_ref[0, t:t + 1, col:col + HEAD_DIM] = (acc_all / l_all).astype(o_ref.dtype)


def _attn_step(q, k_new, v_new, cache_k, cache_v):
    b, t, _ = q.shape
    past = cache_k.shape[1]
    kv_w = N_KV * HEAD_DIM
    assert all(win // dil == BAND and past >= win for win, dil in DIL_GROUPS)
    assert t <= min(d for _, d in DIL_GROUPS if d > 1) and t <= BAND
    views, specs = [], []
    for win, dil in DIL_GROUPS:
        if dil == 1:
            shape = (b, past, kv_w)
            spec = pl.BlockSpec((1, BAND, kv_w), lambda i: (i, past // BAND - 1, 0))
        else:
            shape = (b, past // dil, dil * kv_w)
            blk = (past - win) // dil // BAND
            assert (past - win) % (dil * BAND) == 0 and (dil * kv_w) % (t * kv_w) == 0
            spec = pl.BlockSpec((1, BAND, t * kv_w), lambda i, blk=blk: (i, blk, 0))
        views += [cache_k.reshape(shape), cache_v.reshape(shape)]
        specs += [spec, spec]
    tok = lambda width: pl.BlockSpec((1, t, width), lambda i: (i, 0, 0))
    return pl.pallas_call(
        functools.partial(_attn_step_kernel, T=t),
        grid=(b,),
        in_specs=[tok(q.shape[2]), tok(kv_w), tok(kv_w)] + specs,
        out_specs=tok(N_QH_GROUP * HEAD_DIM),
        out_shape=jax.ShapeDtypeStruct((b, t, N_QH_GROUP * HEAD_DIM), BF16),
        compiler_params=_params("parallel"),
        name="attn_step",
    )(q, k_new, v_new, *views)


FFN_SPLITS = 2


def _post_mixer_kernel(x_ref, o_ref, p_ref, wo_ref, fg_ref, fin_ref, fout_ref, eg_ref, eproj_ref, egate_ref, y_ref):
    x = x_ref[...] + _dot(o_ref[...], wo_ref[...])
    h = _rms(x, fg_ref[...]).astype(BF16)
    width = FFN_HIDDEN // FFN_SPLITS
    ffn = None
    for c in range(FFN_SPLITS):
        a = _dot(h, fin_ref[:, c * width:(c + 1) * width])
        b = _dot(h, fin_ref[:, FFN_HIDDEN + c * width:FFN_HIDDEN + (c + 1) * width])
        part = _dot((a * jax.nn.sigmoid(a) * b).astype(BF16), fout_ref[c * width:(c + 1) * width, :])
        ffn = part if ffn is None else ffn + part
    x = x + ffn
    h = _rms(x, eg_ref[...]).astype(BF16)
    gate = jax.nn.sigmoid(_dot(h, egate_ref[...]))
    y_ref[...] = x + _dot(p_ref[...].astype(BF16), eproj_ref[...]) * gate


def _post_mixer(x, o, p, wo, fg, fin, fout, eg, eproj, egate, tm):
    m = x.shape[0]
    row = lambda width: pl.BlockSpec((tm, width), lambda i: (i, 0))
    return pl.pallas_call(
        _post_mixer_kernel,
        grid=(m // tm,),
        in_specs=[row(D_MODEL), row(o.shape[1]), row(PLE_DIM),
                  _resident(wo.shape), _resident((1, D_MODEL)), _resident(fin.shape), _resident(fout.shape),
                  _resident((1, D_MODEL)), _resident(eproj.shape), _resident(egate.shape)],
        out_specs=row(D_MODEL),
        out_shape=jax.ShapeDtypeStruct((m, D_MODEL), F32),
        compiler_params=_params("parallel"),
        name="post_mixer",
    )(x, o, p, wo, fg, fin, fout, eg, eproj, egate)


def _trunk(x, p, gla_s0, cache, pos, w, *, tm):
    b, l, _ = x.shape
    m = b * l
    depth = p.shape[0]
    n_a = depth // 2
    step = gla_s0 is not None
    xf = x.reshape(m, D_MODEL)
    rope = _rope_tables(pos)
    if step:
        rope = jnp.tile(rope, (1, tm // l, 1))
    states = []
    k_new = v_new = None
    for i in range(depth):
        if i < n_a:
            qk, v, r, glr = _proj_a(xf, w["a_norm"][i], w["a_w_in"][i], F32 if step else BF16, tm)
            if step:
                o, s = _gla_step(qk, v, r, glr, w["a_w_gk_up"][i], w["a_b_gk"][i], w["a_onorm"][i], gla_s0[i], l, 8 // l)
            else:
                o, s = _gla_seq(qk.reshape(b, l, -1), v.reshape(b, l, -1), r.reshape(b, l, -1), glr.reshape(b, l, -1),
                                w["a_w_gk_up"][i], w["a_b_gk"][i], w["a_onorm"][i], T=256)
                o = o.reshape(m, GLA_V)
            states.append(s)
            wo = w["a_w_o"][i]
        else:
            j = i - n_a
            kv_w = N_KV * HEAD_DIM
            if j == 0:
                k_new, v_new = _head_proj(xf, w["kv_norm"], w["w_kv"], w["k_norm"], rope, N_KV,
                                          (kv_w, kv_w), (F32, F32), tm)
            (q,) = _head_proj(xf, w["b_norm"][j], w["b_w_q"][j], w["b_q_norm"][j], rope,
                              N_GROUPS * N_QH_GROUP, (N_GROUPS * N_QH_GROUP * HEAD_DIM,), (F32,), tm)
            if step:
                o = _attn_step(q.reshape(b, l, -1), k_new.reshape(b, l, kv_w), v_new.reshape(b, l, kv_w), *cache)
            else:
                o = _attn_seq(q.reshape(b, l, -1), k_new.reshape(b, l, kv_w), v_new.reshape(b, l, kv_w))
            o = o.reshape(m, N_QH_GROUP * HEAD_DIM)
            wo = w["b_w_o"][j]
        xf = _post_mixer(xf, o, p[i].reshape(m, PLE_DIM), wo, w["f_norm"][i], w["f_w_in"][i], w["f_w_out"][i],
                         w["e_norm"][i], w["e_w_proj"][i], w["e_w_gate"][i], tm)
    y = xf.reshape(b, l, D_MODEL)
    return (y, jnp.stack(states), k_new.reshape(b, l, N_KV, HEAD_DIM), v_new.reshape(b, l, N_KV, HEAD_DIM))


def kernel(x_prompt, x_sample, p_prompt, p_sample, state_gla, cache_k, cache_v, a_norm, a_w_in, a_w_gk_up, a_b_gk,
           a_onorm, a_w_o, kv_norm, w_kv, k_norm, b_norm, b_w_q, b_q_norm, b_w_o, f_norm, f_w_in, f_w_out, e_norm,
           e_w_proj, e_w_gate):
    n_a = a_norm.shape[0]
    depth = f_norm.shape[0]
    main = 2 * GLA_QK + 2 * GLA_V
    a_w_in_p = jnp.concatenate(
        [a_w_in[:, :, :main], jnp.pad(a_w_in[:, :, main:], ((0, 0), (0, 0), (0, GLR_PAD - GLA_LR)))], axis=2)
    w = dict(
        a_norm=a_norm.reshape(n_a, 1, D_MODEL),
        a_w_in=a_w_in_p.astype(BF16),
        a_w_gk_up=jnp.pad(a_w_gk_up, ((0, 0), (0, GLR_PAD - GLA_LR), (0, 0))).astype(BF16),
        a_b_gk=a_b_gk.reshape(n_a, 1, GLA_QK),
        a_onorm=a_onorm.reshape(n_a, 1, GLA_DV),
        a_w_o=a_w_o.astype(BF16),
        kv_norm=kv_norm.reshape(1, D_MODEL),
        w_kv=w_kv.astype(BF16),
        k_norm=k_norm.reshape(1, HEAD_DIM),
        b_norm=b_norm.reshape(-1, 1, D_MODEL),
        b_w_q=b_w_q.astype(BF16),
        b_q_norm=b_q_norm.reshape(-1, 1, HEAD_DIM),
        b_w_o=b_w_o.astype(BF16),
        f_norm=f_norm.reshape(depth, 1, D_MODEL),
        f_w_in=f_w_in.astype(BF16),
        f_w_out=f_w_out.astype(BF16),
        e_norm=e_norm.reshape(depth, 1, D_MODEL),
        e_w_proj=e_w_proj.astype(BF16),
        e_w_gate=e_w_gate.astype(BF16),
    )
    seq = x_prompt.shape[1]
    dec_seq = x_sample.shape[1]
    pos_prompt = jnp.arange(seq, dtype=F32)
    pos_sample = PAST_LEN + jnp.arange(dec_seq, dtype=F32)
    y_p, gs_p, k_p, v_p = _trunk(x_prompt, p_prompt, None, None, pos_prompt, w, tm=256)
    y_s, gs_s, k_s, v_s = _trunk(x_sample, p_sample, state_gla, (cache_k, cache_v), pos_sample, w, tm=256)
    keep = min(MAX_WINDOW, seq)
    return (y_p, y_s, gs_p, gs_s, k_p[:, seq - keep:], v_p[:, seq - keep:], k_s, v_s)
```

```python
import functools
import math

import jax
import jax.numpy as jnp
from jax import lax
from jax.experimental import pallas as pl
from jax.experimental.pallas import tpu as pltpu

D_MODEL = 1024
PAST_LEN = 2048
GLA_HEADS = 4
GLA_DK = 128
GLA_DV = 256
GLA_QK = GLA_HEADS * GLA_DK
GLA_V = GLA_HEADS * GLA_DV
GLA_LR = 16
GLA_TAU = 16.0
GLA_CHUNK = 64
HEAD_DIM = 128
N_KV = 4
Q_PER_KV = 2
N_QH_GROUP = N_KV * Q_PER_KV
DIL_GROUPS = ((128, 1), (512, 4), (2048, 16))
N_GROUPS = len(DIL_GROUPS)
MAX_WINDOW = 2048
ROPE_DIM = HEAD_DIM // 4
ROPE_HALF = ROPE_DIM // 2
ROPE_THETA = 500000.0
FFN_HIDDEN = 2816
PLE_DIM = 256
EPS = 1e-6
NEG_INF = -1e30

LANES = 128
GLR_PAD = LANES
VMEM_LIMIT = 56 * 1024 * 1024

F32 = jnp.float32
BF16 = jnp.bfloat16


def _params(*sem):
    return pltpu.CompilerParams(dimension_semantics=sem, vmem_limit_bytes=VMEM_LIMIT)


def _wspec(lw):
    stack, layer = lw
    index = (layer,) + (0,) * (stack.ndim - 1)
    return pl.BlockSpec((None,) + stack.shape[1:], lambda *_: index, pipeline_mode=pl.Buffered(1))


def _rms(x, g):
    var = jnp.mean(x * x, axis=-1, keepdims=True)
    return x * lax.rsqrt(var + EPS) * g


def _log_sigmoid(x):
    return jnp.minimum(x, 0.0) - jnp.log1p(jnp.exp(-jnp.abs(x)))


def _dot(a, b):
    return jnp.dot(a, b, preferred_element_type=F32)


def _dot_nt(a, b):
    return lax.dot_general(a, b, (((1,), (1,)), ((), ())), preferred_element_type=F32)


def _dot_tn(a, b):
    return lax.dot_general(a, b, (((0,), (0,)), ((), ())), preferred_element_type=F32)


def _proj_a_kernel(x_ref, g_ref, w_ref, qk_ref, v_ref, r_ref, glr_ref):
    h = _rms(x_ref[...], g_ref[...]).astype(BF16)
    qk_ref[...] = _dot(h, w_ref[:, 0:2 * GLA_QK])
    v_ref[...] = _dot(h, w_ref[:, 2 * GLA_QK:2 * GLA_QK + GLA_V]).astype(v_ref.dtype)
    r_ref[...] = _dot(h, w_ref[:, 2 * GLA_QK + GLA_V:2 * GLA_QK + 2 * GLA_V])
    glr_ref[...] = _dot(h, w_ref[:, 2 * GLA_QK + 2 * GLA_V:]).astype(BF16)


def _proj_a(x, g, w, v_dtype, tm):
    m = x.shape[0]
    row = lambda width: pl.BlockSpec((tm, width), lambda i: (i, 0))
    return pl.pallas_call(
        _proj_a_kernel,
        grid=(m // tm,),
        in_specs=[row(D_MODEL), _wspec(g), _wspec(w)],
        out_specs=[row(2 * GLA_QK), row(GLA_V), row(GLA_V), row(GLR_PAD)],
        out_shape=[jax.ShapeDtypeStruct((m, 2 * GLA_QK), F32),
                   jax.ShapeDtypeStruct((m, GLA_V), v_dtype),
                   jax.ShapeDtypeStruct((m, GLA_V), F32),
                   jax.ShapeDtypeStruct((m, GLR_PAD), BF16)],
        compiler_params=_params("parallel"),
        name="proj_a",
    )(x, g[0], w[0])


def _split3(x):
    hi = x.astype(BF16)
    r1 = x - hi.astype(F32)
    mid = r1.astype(BF16)
    lo = (r1 - mid.astype(F32)).astype(BF16)
    return hi, mid, lo


def _gla_seq_kernel(qk_ref, v_ref, r_ref, glr_ref, wgk_ref, bgk_ref, on_ref, o_ref, s_ref, *, T, C):
    @pl.when(pl.program_id(1) == 0)
    def _():
        s_ref[...] = jnp.zeros_like(s_ref)

    logits = _dot(glr_ref[0], wgk_ref[...]) + bgk_ref[...]
    log_a = _log_sigmoid(logits) * (1.0 / GLA_TAU)
    row = lax.broadcasted_iota(jnp.int32, (C, C), 0)
    col = lax.broadcasted_iota(jnp.int32, (C, C), 1)
    causal = col <= row
    tril = jnp.where(causal, 1.0, 0.0).astype(BF16)
    onorm = on_ref[...]
    for h in range(GLA_HEADS):
        state = s_ref[0, h]
        for c in range(T // C):
            rows = slice(c * C, (c + 1) * C)
            la = log_a[rows, h * GLA_DK:(h + 1) * GLA_DK]
            hi, mid, lo = _split3(la)
            cum3 = _dot(tril, jnp.concatenate([hi, mid, lo], axis=1))
            cum = cum3[:, :GLA_DK] + cum3[:, GLA_DK:2 * GLA_DK] + cum3[:, 2 * GLA_DK:]
            last = cum[C - 1:C, :]
            q = qk_ref[0, rows, h * GLA_DK:(h + 1) * GLA_DK] * (GLA_DK ** -0.5)
            k = qk_ref[0, rows, GLA_QK + h * GLA_DK:GLA_QK + (h + 1) * GLA_DK]
            v = v_ref[0, rows, h * GLA_DV:(h + 1) * GLA_DV]
            q_in = (q * jnp.exp(cum)).astype(BF16)
            k_in = (k * jnp.exp(-cum)).astype(BF16)
            k_end = (k * jnp.exp(last - cum)).astype(BF16)
            att = jnp.where(causal, _dot_nt(q_in, k_in), 0.0).astype(BF16)
            o = _dot(att, v) + _dot(q_in, state.astype(BF16))
            decay = jnp.exp(jnp.broadcast_to(last, (GLA_DK, GLA_DK)).T)
            state = state * jnp.concatenate([decay, decay], axis=1) + _dot_tn(k_end, v)
            gate = r_ref[0, rows, h * GLA_DV:(h + 1) * GLA_DV]
            og = _rms(o, onorm) * (gate * jax.nn.sigmoid(gate))
            o_ref[0, rows, h * GLA_DV:(h + 1) * GLA_DV] = og.astype(o_ref.dtype)
        s_ref[0, h] = state


def _gla_seq(qk, v, r, glr, wgk, bgk, onorm, T):
    b, l, _ = qk.shape
    tok = lambda width: pl.BlockSpec((1, T, width), lambda i, t: (i, t, 0))
    return pl.pallas_call(
        functools.partial(_gla_seq_kernel, T=T, C=GLA_CHUNK),
        grid=(b, l // T),
        in_specs=[tok(2 * GLA_QK), tok(GLA_V), tok(GLA_V), tok(GLR_PAD),
                  _wspec(wgk), _wspec(bgk), _wspec(onorm)],
        out_specs=[tok(GLA_V),
                   pl.BlockSpec((1, GLA_HEADS, GLA_DK, GLA_DV), lambda i, t: (i, 0, 0, 0))],
        out_shape=[jax.ShapeDtypeStruct((b, l, GLA_V), BF16),
                   jax.ShapeDtypeStruct((b, GLA_HEADS, GLA_DK, GLA_DV), F32)],
        compiler_params=_params("parallel", "arbitrary"),
        name="gla_seq",
    )(qk, v, r, glr, wgk[0], bgk[0], onorm[0])


def _columns(x):
    pad = jnp.zeros((LANES - x.shape[0], LANES), F32)
    return jnp.concatenate([x, pad], axis=0).T


def _gla_step_kernel(qk_ref, v_ref, r_ref, glr_ref, wgk_ref, bgk_ref, on_ref, s0_ref, o_ref, s_ref, *, nb, L):
    logits = _dot(glr_ref[...], wgk_ref[...]) + bgk_ref[...]
    a = jnp.exp(_log_sigmoid(logits) * (1.0 / GLA_TAU))
    onorm = on_ref[...]
    for h in range(GLA_HEADS):
        lanes = slice(h * GLA_DK, (h + 1) * GLA_DK)
        q_cols = _columns(qk_ref[:, lanes] * (GLA_DK ** -0.5))
        k_cols = _columns(qk_ref[:, GLA_QK + h * GLA_DK:GLA_QK + (h + 1) * GLA_DK])
        a_cols = _columns(a[:, lanes])
        outs = []
        for j in range(nb):
            state = s0_ref[j, h]
            for t in range(L):
                i = j * L + t
                v_row = v_ref[i:i + 1, h * GLA_DV:(h + 1) * GLA_DV]
                state = a_cols[:, i:i + 1] * state + k_cols[:, i:i + 1] * v_row
                outs.append(jnp.sum(q_cols[:, i:i + 1] * state, axis=0, keepdims=True))
            s_ref[j, h] = state
        o = jnp.concatenate(outs, axis=0)
        gate = r_ref[:, h * GLA_DV:(h + 1) * GLA_DV]
        og = _rms(o, onorm) * (gate * jax.nn.sigmoid(gate))
        o_ref[:, h * GLA_DV:(h + 1) * GLA_DV] = og.astype(o_ref.dtype)


def _gla_step(qk, v, r, glr, wgk, bgk, onorm, s0, L, nb):
    m = qk.shape[0]
    b = m // L
    rows = nb * L
    tok = lambda width: pl.BlockSpec((rows, width), lambda i: (i, 0))
    st = pl.BlockSpec((nb, GLA_HEADS, GLA_DK, GLA_DV), lambda i: (i, 0, 0, 0))
    s0_stack, layer = s0
    st_in = pl.BlockSpec((None, nb, GLA_HEADS, GLA_DK, GLA_DV), lambda i: (layer, i, 0, 0, 0))
    return pl.pallas_call(
        functools.partial(_gla_step_kernel, nb=nb, L=L),
        grid=(b // nb,),
        in_specs=[tok(2 * GLA_QK), tok(GLA_V), tok(GLA_V), tok(GLR_PAD),
                  _wspec(wgk), _wspec(bgk), _wspec(onorm), st_in],
        out_specs=[tok(GLA_V), st],
        out_shape=[jax.ShapeDtypeStruct((m, GLA_V), BF16),
                   jax.ShapeDtypeStruct((b, GLA_HEADS, GLA_DK, GLA_DV), F32)],
        compiler_params=_params("parallel"),
        name="gla_step",
    )(qk, v, r, glr, wgk[0], bgk[0], onorm[0], s0_stack)


def _head_proj_kernel(x_ref, g_ref, w_ref, hg_ref, rope_ref, *out_refs, n_rot, outs):
    h = _rms(x_ref[...], g_ref[...]).astype(BF16)
    y = _dot(h, w_ref[...])
    hg = hg_ref[...]
    cos, sin_lo, sin_hi = rope_ref[0], rope_ref[1], rope_ref[2]
    heads = []
    for i in range(y.shape[1] // HEAD_DIM):
        yh = y[:, i * HEAD_DIM:(i + 1) * HEAD_DIM]
        if i < n_rot:
            yn = _rms(yh, hg)
            yh = (yn * cos + pltpu.roll(yn, HEAD_DIM - ROPE_HALF, axis=1) * sin_lo
                  + pltpu.roll(yn, ROPE_HALF, axis=1) * sin_hi)
        heads.append(yh)
    for ref, (first, count, by_head) in zip(out_refs, outs):
        if by_head:
            for i in range(count):
                ref[:, i, :] = heads[first + i].astype(ref.dtype)
        else:
            ref[...] = jnp.concatenate(heads[first:first + count], axis=1).astype(ref.dtype)


def _head_proj(x, g, w, hg, rope, n_rot, outs, dtypes, tm):
    m = x.shape[0]
    n_rope_tiles = rope.shape[1] // tm
    row = lambda width: pl.BlockSpec((tm, width), lambda i: (i, 0))
    out_specs, out_shape = [], []
    for (first, count, by_head), dt in zip(outs, dtypes):
        if by_head:
            out_specs.append(pl.BlockSpec((tm, count, HEAD_DIM), lambda i: (i, 0, 0)))
            out_shape.append(jax.ShapeDtypeStruct((m, count, HEAD_DIM), dt))
        else:
            out_specs.append(row(count * HEAD_DIM))
            out_shape.append(jax.ShapeDtypeStruct((m, count * HEAD_DIM), dt))
    return pl.pallas_call(
        functools.partial(_head_proj_kernel, n_rot=n_rot, outs=outs),
        grid=(m // tm,),
        in_specs=[row(D_MODEL), _wspec(g), _wspec(w), _wspec(hg),
                  pl.BlockSpec((3, tm, HEAD_DIM), lambda i: (0, i % n_rope_tiles, 0))],
        out_specs=out_specs,
        out_shape=out_shape,
        compiler_params=_params("parallel"),
        name="head_proj",
    )(x, g[0], w[0], hg[0], rope)


def _rope_tables(pos):
    freqs = jnp.power(ROPE_THETA, -jnp.arange(ROPE_HALF, dtype=F32) * 2.0 / ROPE_DIM)
    ang = pos[:, None] * freqs[None, :]
    cos, sin = jnp.cos(ang), jnp.sin(ang)
    n = pos.shape[0]
    rest = HEAD_DIM - ROPE_DIM
    c = jnp.concatenate([cos, cos, jnp.ones((n, rest), F32)], axis=1)
    s_lo = jnp.concatenate([-sin, jnp.zeros((n, ROPE_HALF + rest), F32)], axis=1)
    s_hi = jnp.concatenate([jnp.zeros((n, ROPE_HALF), F32), sin, jnp.zeros((n, rest), F32)], axis=1)
    return jnp.stack([c, s_lo, s_hi])


BAND = 128


def _attn_seq_kernel(*refs, L):
    n_q = N_GROUPS * Q_PER_KV
    q_refs = refs[:n_q]
    k_ref, v_ref, o_ref, acc_ref, m_ref, l_ref = refs[n_q:]
    scale = HEAD_DIM ** -0.5
    n2 = Q_PER_KV * BAND
    qi = lax.broadcasted_iota(jnp.int32, (n2, 2 * BAND), 0) % BAND
    kj = lax.broadcasted_iota(jnp.int32, (n2, 2 * BAND), 1)
    band_mask = (kj >= qi) & (kj <= qi + BAND)
    first_mask = (lax.broadcasted_iota(jnp.int32, (n2, BAND), 1)
                  <= lax.broadcasted_iota(jnp.int32, (n2, BAND), 0) % BAND)

    for g, (win, dil) in enumerate(DIL_GROUPS):
        assert win // dil == BAND
        for r in range(dil):
            for blk in range(L // dil // BAND):
                q_rows = pl.ds(r + dil * BAND * blk, BAND, stride=dil)
                q = jnp.concatenate([q_refs[g * Q_PER_KV + j][0, q_rows, :]
                                     for j in range(Q_PER_KV)], axis=0).astype(BF16)
                if blk == 0:
                    k_rows, mask = q_rows, first_mask
                else:
                    k_rows = pl.ds(r + dil * BAND * (blk - 1), 2 * BAND, stride=dil)
                    mask = band_mask
                kb = k_ref[0, k_rows, :].astype(BF16)
                vb = v_ref[0, k_rows, :].astype(BF16)
                s = jnp.where(mask, _dot_nt(q, kb) * scale, NEG_INF)
                for j in range(Q_PER_KV):
                    s_j = s[j * BAND:(j + 1) * BAND]
                    m_blk = jnp.max(s_j, axis=-1, keepdims=True)
                    if g == 0:
                        m_new = jnp.broadcast_to(m_blk, (BAND, LANES))
                    else:
                        m_old = m_ref[j, q_rows, :]
                        m_new = jnp.maximum(m_old, m_blk)
                    p = jnp.exp(s_j - m_new[:, :1])
                    psum = jnp.sum(p, axis=-1, keepdims=True)
                    pv = _dot(p.astype(BF16), vb)
                    if g == 0:
                        l_new = jnp.broadcast_to(psum, (BAND, LANES))
                        acc_new = pv
                    else:
                        alpha = jnp.exp(m_old - m_new)
                        l_new = alpha * l_ref[j, q_rows, :] + psum
                        acc_new = alpha * acc_ref[j, q_rows, :] + pv
                    m_ref[j, q_rows, :] = m_new
                    l_ref[j, q_rows, :] = l_new
                    acc_ref[j, q_rows, :] = acc_new
    for j in range(Q_PER_KV):
        o_ref[0, :, j * HEAD_DIM:(j + 1) * HEAD_DIM] = (acc_ref[j] / l_ref[j]).astype(o_ref.dtype)


def _attn_seq(q, k, v):
    b, l, _ = q.shape
    pair = Q_PER_KV * HEAD_DIM
    q_spec = lambda g, j: pl.BlockSpec((1, l, HEAD_DIM), lambda i, h: (i, 0, (g * N_KV + h) * Q_PER_KV + j))
    q_specs = [q_spec(g, j) for g in range(N_GROUPS) for j in range(Q_PER_KV)]
    kv_spec = pl.BlockSpec((1, l, HEAD_DIM), lambda i, h: (i, 0, h))
    stat = pltpu.VMEM((Q_PER_KV, l, LANES), F32)
    return pl.pallas_call(
        functools.partial(_attn_seq_kernel, L=l),
        grid=(b, N_KV),
        in_specs=q_specs + [kv_spec, kv_spec],
        out_specs=pl.BlockSpec((1, l, pair), lambda i, h: (i, 0, h)),
        out_shape=jax.ShapeDtypeStruct((b, l, N_QH_GROUP * HEAD_DIM), BF16),
        scratch_shapes=[stat, stat, stat],
        compiler_params=_params("parallel", "arbitrary"),
        name="attn_seq",
    )(*([q] * len(q_specs)), k, v)


def _attn_step_kernel(q_ref, kn_ref, vn_ref, k0_ref, v0_ref, k1_ref, v1_ref, k2_ref, v2_ref, o_ref, *, T, nb):
    scale = HEAD_DIM ** -0.5
    tile_refs = ((k0_ref, v0_ref), (k1_ref, v1_ref), (k2_ref, v2_ref))
    key_idx = lax.broadcasted_iota(jnp.int32, (BAND, 1), 0)
    new_idx = lax.broadcasted_iota(jnp.int32, (T, 1), 0)
    for bj in range(nb):
        for h in range(N_KV):
            new_rows = pl.ds(bj * T * N_KV + h, T, stride=N_KV)
            k_new = kn_ref[new_rows, :]
            v_new = vn_ref[new_rows, :]
            for t in range(T):
                row = bj * T + t
                stats = [[] for _ in range(Q_PER_KV)]
                for g, (win, dil) in enumerate(DIL_GROUPS):
                    k_ref, v_ref = tile_refs[g]
                    if dil == 1:
                        rows = pl.ds(h, BAND, stride=N_KV)
                        k_tile, v_tile = k_ref[bj, rows, :], v_ref[bj, rows, :]
                        tile_ok = key_idx >= t
                        new_ok = new_idx <= t
                    else:
                        k_tile, v_tile = k_ref[bj, :, t * N_KV + h, :], v_ref[bj, :, t * N_KV + h, :]
                        tile_ok = None
                        new_ok = new_idx == t
                    for j in range(Q_PER_KV):
                        col = ((g * N_KV + h) * Q_PER_KV + j) * HEAD_DIM
                        q = q_ref[row:row + 1, col:col + HEAD_DIM]
                        s_tile = jnp.sum(k_tile * q, axis=-1, keepdims=True) * scale
                        if tile_ok is not None:
                            s_tile = jnp.where(tile_ok, s_tile, NEG_INF)
                        s_new = jnp.where(new_ok, jnp.sum(k_new * q, axis=-1, keepdims=True) * scale, NEG_INF)
                        m = jnp.maximum(jnp.max(s_tile, axis=0, keepdims=True),
                                        jnp.max(s_new, axis=0, keepdims=True))
                        p_tile = jnp.exp(s_tile - m)
                        p_new = jnp.exp(s_new - m)
                        l = jnp.sum(p_tile, axis=0, keepdims=True) + jnp.sum(p_new, axis=0, keepdims=True)
                        acc = (jnp.sum(p_tile * v_tile, axis=0, keepdims=True)
                               + jnp.sum(p_new * v_new, axis=0, keepdims=True))
                        stats[j].append((m, l, acc))
                for j in range(Q_PER_KV):
                    m_all = functools.reduce(jnp.maximum, [m for m, _, _ in stats[j]])
                    l_all = sum(l * jnp.exp(m - m_all) for m, l, _ in stats[j])
                    acc_all = sum(acc * jnp.exp(m - m_all) for m, _, acc in stats[j])
                    col = (h * Q_PER_KV + j) * HEAD_DIM
                    o_ref[row:row + 1, col:col + HEAD_DIM] = (acc_all / l_all).astype(o_ref.dtype)


def _attn_step(q, k_new, v_new, cache_k, cache_v, t, nb):
    m = q.shape[0]
    b = m // t
    past = cache_k.shape[1]
    assert all(win // dil == BAND and past >= win for win, dil in DIL_GROUPS)
    assert t <= min(d for _, d in DIL_GROUPS if d > 1) and t <= BAND
    views, specs = [], []
    for win, dil in DIL_GROUPS:
        if dil == 1:
            shape = (b, past * N_KV, HEAD_DIM)
            spec = pl.BlockSpec((nb, BAND * N_KV, HEAD_DIM), lambda i: (i, past // BAND - 1, 0))
        else:
            shape = (b, past // dil, dil * N_KV, HEAD_DIM)
            blk = (past - win) // (dil * BAND)
            assert (past - win) % (dil * BAND) == 0
            spec = pl.BlockSpec((nb, BAND, t * N_KV, HEAD_DIM), lambda i, blk=blk: (i, blk, 0, 0))
        views += [cache_k.reshape(shape), cache_v.reshape(shape)]
        specs += [spec, spec]
    tok = lambda width: pl.BlockSpec((nb * t, width), lambda i: (i, 0))
    new = pl.BlockSpec((nb * t * N_KV, HEAD_DIM), lambda i: (i, 0))
    return pl.pallas_call(
        functools.partial(_attn_step_kernel, T=t, nb=nb),
        grid=(b // nb,),
        in_specs=[tok(q.shape[1]), new, new] + specs,
        out_specs=tok(N_QH_GROUP * HEAD_DIM),
        out_shape=jax.ShapeDtypeStruct((m, N_QH_GROUP * HEAD_DIM), BF16),
        compiler_params=_params("parallel"),
        name="attn_step",
    )(q, k_new.reshape(m * N_KV, HEAD_DIM), v_new.reshape(m * N_KV, HEAD_DIM), *views)


FFN_SPLITS = 2


def _post_mixer_kernel(x_ref, o_ref, p_ref, wo_ref, fg_ref, fin_ref, fout_ref, eg_ref, eproj_ref, egate_ref, y_ref):
    x = x_ref[...] + _dot(o_ref[...], wo_ref[...])
    h = _rms(x, fg_ref[...]).astype(BF16)
    width = FFN_HIDDEN // FFN_SPLITS
    ffn = None
    for c in range(FFN_SPLITS):
        a = _dot(h, fin_ref[:, c * width:(c + 1) * width])
        b = _dot(h, fin_ref[:, FFN_HIDDEN + c * width:FFN_HIDDEN + (c + 1) * width])
        part = _dot((a * jax.nn.sigmoid(a) * b).astype(BF16), fout_ref[c * width:(c + 1) * width, :])
        ffn = part if ffn is None else ffn + part
    x = x + ffn
    h = _rms(x, eg_ref[...]).astype(BF16)
    gate = jax.nn.sigmoid(_dot(h, egate_ref[...]))
    y_ref[...] = x + _dot(p_ref[...].astype(BF16), eproj_ref[...]) * gate


def _post_mixer(x, o, p, wo, fg, fin, fout, eg, eproj, egate, tm):
    m = x.shape[0]
    row = lambda width: pl.BlockSpec((tm, width), lambda i: (i, 0))
    p_stack, layer = p
    return pl.pallas_call(
        _post_mixer_kernel,
        grid=(m // tm,),
        in_specs=[row(D_MODEL), row(o.shape[1]), pl.BlockSpec((None, tm, PLE_DIM), lambda i: (layer, i, 0)),
                  _wspec(wo), _wspec(fg), _wspec(fin), _wspec(fout), _wspec(eg), _wspec(eproj), _wspec(egate)],
        out_specs=row(D_MODEL),
        out_shape=jax.ShapeDtypeStruct((m, D_MODEL), F32),
        compiler_params=_params("parallel"),
        name="post_mixer",
    )(x, o, p_stack, wo[0], fg[0], fin[0], fout[0], eg[0], eproj[0], egate[0])


def _trunk(x, p, gla_s0, cache, pos, w, *, tm):
    b, l, _ = x.shape
    m = b * l
    depth = p.shape[0]
    n_a = depth // 2
    step = gla_s0 is not None
    xf = x.reshape(m, D_MODEL)
    pf = p.reshape(depth, m, PLE_DIM)
    rope = _rope_tables(pos)
    if step:
        rope = jnp.tile(rope, (1, tm // l, 1))
    lw = lambda name, layer: (w[name], layer)
    states = []
    k_new = v_new = None
    for i in range(depth):
        if i < n_a:
            qk, v, r, glr = _proj_a(xf, lw("a_norm", i), lw("a_w_in", i), F32 if step else BF16, tm)
            gate_w = (lw("a_w_gk_up", i), lw("a_b_gk", i), lw("a_onorm", i))
            if step:
                o, s = _gla_step(qk, v, r, glr, *gate_w, (gla_s0, i), l, 8 // l)
            else:
                o, s = _gla_seq(qk.reshape(b, l, -1), v.reshape(b, l, -1), r.reshape(b, l, -1), glr.reshape(b, l, -1),
                                *gate_w, T=256)
                o = o.reshape(m, GLA_V)
            states.append(s)
            wo = lw("a_w_o", i)
        else:
            j = i - n_a
            kv_w = N_KV * HEAD_DIM
            if j == 0:
                by_head = [(0, N_KV, True), (N_KV, N_KV, True)]
                flat = [] if step else [(0, N_KV, False), (N_KV, N_KV, False)]
                kv = _head_proj(xf, lw("kv_norm", 0), lw("w_kv", 0), lw("k_norm", 0), rope, N_KV,
                                by_head + flat, (F32,) * (2 + len(flat)), tm)
                k_new, v_new = kv[:2]
            n_q = N_GROUPS * N_QH_GROUP
            (q,) = _head_proj(xf, lw("b_norm", j), lw("b_w_q", j), lw("b_q_norm", j), rope, n_q,
                              [(0, n_q, False)], (F32,), tm)
            if step:
                o = _attn_step(q, k_new, v_new, *cache, l, 8 // l)
            else:
                o = _attn_seq(q.reshape(b, l, -1), kv[2].reshape(b, l, kv_w), kv[3].reshape(b, l, kv_w))
                o = o.reshape(m, N_QH_GROUP * HEAD_DIM)
            wo = lw("b_w_o", j)
        xf = _post_mixer(xf, o, (pf, i), wo, lw("f_norm", i), lw("f_w_in", i), lw("f_w_out", i),
                         lw("e_norm", i), lw("e_w_proj", i), lw("e_w_gate", i), tm)
    y = xf.reshape(b, l, D_MODEL)
    return (y, jnp.stack(states), k_new.reshape(b, l, N_KV, HEAD_DIM), v_new.reshape(b, l, N_KV, HEAD_DIM))


def kernel(x_prompt, x_sample, p_prompt, p_sample, state_gla, cache_k, cache_v, a_norm, a_w_in, a_w_gk_up, a_b_gk,
           a_onorm, a_w_o, kv_norm, w_kv, k_norm, b_norm, b_w_q, b_q_norm, b_w_o, f_norm, f_w_in, f_w_out, e_norm,
           e_w_proj, e_w_gate):
    n_a = a_norm.shape[0]
    depth = f_norm.shape[0]
    main = 2 * GLA_QK + 2 * GLA_V
    a_w_in_p = jnp.concatenate(
        [a_w_in[:, :, :main], jnp.pad(a_w_in[:, :, main:], ((0, 0), (0, 0), (0, GLR_PAD - GLA_LR)))], axis=2)
    w = dict(
        a_norm=a_norm.reshape(n_a, 1, D_MODEL),
        a_w_in=a_w_in_p.astype(BF16),
        a_w_gk_up=jnp.pad(a_w_gk_up, ((0, 0), (0, GLR_PAD - GLA_LR), (0, 0))).astype(BF16),
        a_b_gk=a_b_gk.reshape(n_a, 1, GLA_QK),
        a_onorm=a_onorm.reshape(n_a, 1, GLA_DV),
        a_w_o=a_w_o.astype(BF16),
        kv_norm=kv_norm.reshape(1, 1, D_MODEL),
        w_kv=w_kv.astype(BF16)[None],
        k_norm=k_norm.reshape(1, 1, HEAD_DIM),
        b_norm=b_norm.reshape(-1, 1, D_MODEL),
        b_w_q=b_w_q.astype(BF16),
        b_q_norm=b_q_norm.reshape(-1, 1, HEAD_DIM),
        b_w_o=b_w_o.astype(BF16),
        f_norm=f_norm.reshape(depth, 1, D_MODEL),
        f_w_in=f_w_in.astype(BF16),
        f_w_out=f_w_out.astype(BF16),
        e_norm=e_norm.reshape(depth, 1, D_MODEL),
        e_w_proj=e_w_proj.astype(BF16),
        e_w_gate=e_w_gate.astype(BF16),
    )
    seq = x_prompt.shape[1]
    dec_seq = x_sample.shape[1]
    pos_prompt = jnp.arange(seq, dtype=F32)
    pos_sample = PAST_LEN + jnp.arange(dec_seq, dtype=F32)
    y_p, gs_p, k_p, v_p = _trunk(x_prompt, p_prompt, None, None, pos_prompt, w, tm=256)
    y_s, gs_s, k_s, v_s = _trunk(x_sample, p_sample, state_gla, (cache_k, cache_v), pos_sample, w, tm=256)
    keep = min(MAX_WINDOW, seq)
    return (y_p, y_s, gs_p, gs_s, k_p[:, seq - keep:], v_p[:, seq - keep:], k_s, v_s)
```

```python
import functools
import math

import jax
import jax.numpy as jnp
import numpy as np
from jax import lax
from jax.experimental import pallas as pl
from jax.experimental.pallas import tpu as pltpu

D_MODEL = 1024
PAST_LEN = 2048
GLA_HEADS = 4
GLA_DK = 128
GLA_DV = 256
GLA_QK = GLA_HEADS * GLA_DK
GLA_V = GLA_HEADS * GLA_DV
GLA_LR = 16
GLA_TAU = 16.0
GLA_CHUNK = 64
HEAD_DIM = 128
N_KV = 4
Q_PER_KV = 2
N_QH_GROUP = N_KV * Q_PER_KV
DIL_GROUPS = ((128, 1), (512, 4), (2048, 16))
N_GROUPS = len(DIL_GROUPS)
MAX_WINDOW = 2048
ROPE_DIM = HEAD_DIM // 4
ROPE_HALF = ROPE_DIM // 2
ROPE_THETA = 500000.0
FFN_HIDDEN = 2816
PLE_DIM = 256
EPS = 1e-6
NEG_INF = -1e30

LANES = 128
GLR_PAD = LANES
VMEM_LIMIT = 56 * 1024 * 1024

F32 = jnp.float32
BF16 = jnp.bfloat16


def _params(*sem):
    return pltpu.CompilerParams(dimension_semantics=sem, vmem_limit_bytes=VMEM_LIMIT)


def _wspec(lw):
    stack, layer = lw
    index = (layer,) + (0,) * (stack.ndim - 1)
    return pl.BlockSpec((None,) + stack.shape[1:], lambda *_: index, pipeline_mode=pl.Buffered(1))


def _rms(x, g):
    var = jnp.mean(x * x, axis=-1, keepdims=True)
    return x * lax.rsqrt(var + EPS) * g


def _log_sigmoid(x):
    return jnp.minimum(x, 0.0) - jnp.log1p(jnp.exp(-jnp.abs(x)))


def _dot(a, b):
    return jnp.dot(a, b, preferred_element_type=F32)


def _dot_nt(a, b):
    return lax.dot_general(a, b, (((1,), (1,)), ((), ())), preferred_element_type=F32)


def _dot_tn(a, b):
    return lax.dot_general(a, b, (((0,), (0,)), ((), ())), preferred_element_type=F32)


def _proj_a_kernel(x_ref, g_ref, w_ref, qk_ref, v_ref, r_ref, glr_ref):
    h = _rms(x_ref[...], g_ref[...]).astype(BF16)
    qk_ref[...] = _dot(h, w_ref[:, 0:2 * GLA_QK])
    v_ref[...] = _dot(h, w_ref[:, 2 * GLA_QK:2 * GLA_QK + GLA_V]).astype(v_ref.dtype)
    r_ref[...] = _dot(h, w_ref[:, 2 * GLA_QK + GLA_V:2 * GLA_QK + 2 * GLA_V])
    glr_ref[...] = _dot(h, w_ref[:, 2 * GLA_QK + 2 * GLA_V:]).astype(BF16)


def _proj_a(x, g, w, v_dtype, tm):
    m = x.shape[0]
    row = lambda width: pl.BlockSpec((tm, width), lambda i: (i, 0))
    return pl.pallas_call(
        _proj_a_kernel,
        grid=(m // tm,),
        in_specs=[row(D_MODEL), _wspec(g), _wspec(w)],
        out_specs=[row(2 * GLA_QK), row(GLA_V), row(GLA_V), row(GLR_PAD)],
        out_shape=[jax.ShapeDtypeStruct((m, 2 * GLA_QK), F32),
                   jax.ShapeDtypeStruct((m, GLA_V), v_dtype),
                   jax.ShapeDtypeStruct((m, GLA_V), F32),
                   jax.ShapeDtypeStruct((m, GLR_PAD), BF16)],
        compiler_params=_params("parallel"),
        name="proj_a",
    )(x, g[0], w[0])


def _split3(x):
    hi = x.astype(BF16)
    r1 = x - hi.astype(F32)
    mid = r1.astype(BF16)
    lo = (r1 - mid.astype(F32)).astype(BF16)
    return hi, mid, lo


def _gla_seq_kernel(qk_ref, v_ref, r_ref, glr_ref, wgk_ref, bgk_ref, on_ref, o_ref, s_ref, *, T, C):
    @pl.when(pl.program_id(1) == 0)
    def _():
        s_ref[...] = jnp.zeros_like(s_ref)

    logits = _dot(glr_ref[0], wgk_ref[...]) + bgk_ref[...]
    log_a = _log_sigmoid(logits) * (1.0 / GLA_TAU)
    row = lax.broadcasted_iota(jnp.int32, (T, T), 0)
    col = lax.broadcasted_iota(jnp.int32, (T, T), 1)
    same_chunk = (row // C) == (col // C)
    causal = same_chunk & (col <= row)
    tril = jnp.where(causal, 1.0, 0.0).astype(BF16)
    chunk_ones = jnp.where(same_chunk, 1.0, 0.0).astype(BF16)
    parts = jnp.concatenate(_split3(log_a), axis=1)
    fold = lambda y: y[:, :GLA_QK] + y[:, GLA_QK:2 * GLA_QK] + y[:, 2 * GLA_QK:]
    cum = fold(_dot(tril, parts))
    total = fold(_dot(chunk_ones, parts))
    q = qk_ref[0, :, :GLA_QK] * (GLA_DK ** -0.5)
    k = qk_ref[0, :, GLA_QK:]
    q_in = (q * jnp.exp(cum)).astype(BF16)
    k_in = (k * jnp.exp(-cum)).astype(BF16)
    k_end = (k * jnp.exp(total - cum)).astype(BF16)
    onorm = on_ref[...]
    for h in range(GLA_HEADS):
        lanes = slice(h * GLA_DK, (h + 1) * GLA_DK)
        v = v_ref[0, :, h * GLA_DV:(h + 1) * GLA_DV]
        att = jnp.where(causal, _dot_nt(q_in[:, lanes], k_in[:, lanes]), 0.0).astype(BF16)
        o_intra = _dot(att, v)
        state = s_ref[0, h]
        o_inter = []
        for c in range(T // C):
            rows = slice(c * C, (c + 1) * C)
            o_inter.append(_dot(q_in[rows, lanes], state.astype(BF16)))
            decay = jnp.exp(jnp.broadcast_to(total[c * C:c * C + 1, lanes], (GLA_DK, GLA_DK)).T)
            state = state * jnp.concatenate([decay, decay], axis=1) + _dot_tn(k_end[rows, lanes], v[rows])
        s_ref[0, h] = state
        o = o_intra + jnp.concatenate(o_inter, axis=0)
        gate = r_ref[0, :, h * GLA_DV:(h + 1) * GLA_DV]
        og = _rms(o, onorm) * (gate * jax.nn.sigmoid(gate))
        o_ref[0, :, h * GLA_DV:(h + 1) * GLA_DV] = og.astype(o_ref.dtype)


def _gla_seq(qk, v, r, glr, wgk, bgk, onorm, T):
    b, l, _ = qk.shape
    tok = lambda width: pl.BlockSpec((1, T, width), lambda i, t: (i, t, 0))
    return pl.pallas_call(
        functools.partial(_gla_seq_kernel, T=T, C=GLA_CHUNK),
        grid=(b, l // T),
        in_specs=[tok(2 * GLA_QK), tok(GLA_V), tok(GLA_V), tok(GLR_PAD),
                  _wspec(wgk), _wspec(bgk), _wspec(onorm)],
        out_specs=[tok(GLA_V),
                   pl.BlockSpec((1, GLA_HEADS, GLA_DK, GLA_DV), lambda i, t: (i, 0, 0, 0))],
        out_shape=[jax.ShapeDtypeStruct((b, l, GLA_V), BF16),
                   jax.ShapeDtypeStruct((b, GLA_HEADS, GLA_DK, GLA_DV), F32)],
        compiler_params=_params("parallel", "arbitrary"),
        name="gla_seq",
    )(qk, v, r, glr, wgk[0], bgk[0], onorm[0])


def _columns(x):
    pad = jnp.zeros((LANES - x.shape[0], LANES), F32)
    return jnp.concatenate([x, pad], axis=0).T


def _gla_step_kernel(qk_ref, v_ref, r_ref, glr_ref, wgk_ref, bgk_ref, on_ref, s0_ref, o_ref, s_ref, *, nb, L):
    logits = _dot(glr_ref[...], wgk_ref[...]) + bgk_ref[...]
    a = jnp.exp(_log_sigmoid(logits) * (1.0 / GLA_TAU))
    onorm = on_ref[...]
    for h in range(GLA_HEADS):
        lanes = slice(h * GLA_DK, (h + 1) * GLA_DK)
        q_cols = _columns(qk_ref[:, lanes] * (GLA_DK ** -0.5))
        k_cols = _columns(qk_ref[:, GLA_QK + h * GLA_DK:GLA_QK + (h + 1) * GLA_DK])
        a_cols = _columns(a[:, lanes])
        outs = []
        for j in range(nb):
            state = s0_ref[j, h]
            for t in range(L):
                i = j * L + t
                v_row = v_ref[i:i + 1, h * GLA_DV:(h + 1) * GLA_DV]
                state = a_cols[:, i:i + 1] * state + k_cols[:, i:i + 1] * v_row
                outs.append(jnp.sum(q_cols[:, i:i + 1] * state, axis=0, keepdims=True))
            s_ref[j, h] = state
        o = jnp.concatenate(outs, axis=0)
        gate = r_ref[:, h * GLA_DV:(h + 1) * GLA_DV]
        og = _rms(o, onorm) * (gate * jax.nn.sigmoid(gate))
        o_ref[:, h * GLA_DV:(h + 1) * GLA_DV] = og.astype(o_ref.dtype)


def _gla_step(qk, v, r, glr, wgk, bgk, onorm, s0, L, nb):
    m = qk.shape[0]
    b = m // L
    rows = nb * L
    tok = lambda width: pl.BlockSpec((rows, width), lambda i: (i, 0))
    st = pl.BlockSpec((nb, GLA_HEADS, GLA_DK, GLA_DV), lambda i: (i, 0, 0, 0))
    s0_stack, layer = s0
    st_in = pl.BlockSpec((None, nb, GLA_HEADS, GLA_DK, GLA_DV), lambda i: (layer, i, 0, 0, 0))
    return pl.pallas_call(
        functools.partial(_gla_step_kernel, nb=nb, L=L),
        grid=(b // nb,),
        in_specs=[tok(2 * GLA_QK), tok(GLA_V), tok(GLA_V), tok(GLR_PAD),
                  _wspec(wgk), _wspec(bgk), _wspec(onorm), st_in],
        out_specs=[tok(GLA_V), st],
        out_shape=[jax.ShapeDtypeStruct((m, GLA_V), BF16),
                   jax.ShapeDtypeStruct((b, GLA_HEADS, GLA_DK, GLA_DV), F32)],
        compiler_params=_params("parallel"),
        name="gla_step",
    )(qk, v, r, glr, wgk[0], bgk[0], onorm[0], s0_stack)


def _head_proj_kernel(x_ref, g_ref, w_ref, hg_ref, rope_ref, *out_refs, n_rot, outs):
    h = _rms(x_ref[...], g_ref[...]).astype(BF16)
    y = _dot(h, w_ref[...])
    hg = hg_ref[...]
    cos, sin_lo, sin_hi = rope_ref[0], rope_ref[1], rope_ref[2]
    heads = []
    for i in range(y.shape[1] // HEAD_DIM):
        yh = y[:, i * HEAD_DIM:(i + 1) * HEAD_DIM]
        if i < n_rot:
            yn = _rms(yh, hg)
            yh = (yn * cos + pltpu.roll(yn, HEAD_DIM - ROPE_HALF, axis=1) * sin_lo
                  + pltpu.roll(yn, ROPE_HALF, axis=1) * sin_hi)
        heads.append(yh)
    for ref, (first, count, by_head) in zip(out_refs, outs):
        if by_head:
            for i in range(count):
                ref[:, i, :] = heads[first + i].astype(ref.dtype)
        else:
            ref[...] = jnp.concatenate(heads[first:first + count], axis=1).astype(ref.dtype)


def _head_proj(x, g, w, hg, rope, n_rot, outs, dtypes, tm):
    m = x.shape[0]
    n_rope_tiles = rope.shape[1] // tm
    row = lambda width: pl.BlockSpec((tm, width), lambda i: (i, 0))
    out_specs, out_shape = [], []
    for (first, count, by_head), dt in zip(outs, dtypes):
        if by_head:
            out_specs.append(pl.BlockSpec((tm, count, HEAD_DIM), lambda i: (i, 0, 0)))
            out_shape.append(jax.ShapeDtypeStruct((m, count, HEAD_DIM), dt))
        else:
            out_specs.append(row(count * HEAD_DIM))
            out_shape.append(jax.ShapeDtypeStruct((m, count * HEAD_DIM), dt))
    return pl.pallas_call(
        functools.partial(_head_proj_kernel, n_rot=n_rot, outs=outs),
        grid=(m // tm,),
        in_specs=[row(D_MODEL), _wspec(g), _wspec(w), _wspec(hg),
                  pl.BlockSpec((3, tm, HEAD_DIM), lambda i: (0, i % n_rope_tiles, 0))],
        out_specs=out_specs,
        out_shape=out_shape,
        compiler_params=_params("parallel"),
        name="head_proj",
    )(x, g[0], w[0], hg[0], rope)


def _rope_tables(pos):
    freqs = jnp.power(ROPE_THETA, -jnp.arange(ROPE_HALF, dtype=F32) * 2.0 / ROPE_DIM)
    ang = pos[:, None] * freqs[None, :]
    cos, sin = jnp.cos(ang), jnp.sin(ang)
    n = pos.shape[0]
    rest = HEAD_DIM - ROPE_DIM
    c = jnp.concatenate([cos, cos, jnp.ones((n, rest), F32)], axis=1)
    s_lo = jnp.concatenate([-sin, jnp.zeros((n, ROPE_HALF + rest), F32)], axis=1)
    s_hi = jnp.concatenate([jnp.zeros((n, ROPE_HALF), F32), sin, jnp.zeros((n, rest), F32)], axis=1)
    return jnp.stack([c, s_lo, s_hi])


BAND = 128


def _attn_seq_kernel(*refs, L):
    n_q = N_GROUPS * Q_PER_KV
    q_refs = refs[:n_q]
    k_ref, v_ref, o_ref, og_ref, lse_ref = refs[n_q:]
    scale = HEAD_DIM ** -0.5
    n2 = Q_PER_KV * BAND
    qi = lax.broadcasted_iota(jnp.int32, (n2, 2 * BAND), 0) % BAND
    kj = lax.broadcasted_iota(jnp.int32, (n2, 2 * BAND), 1)
    band_mask = (kj >= qi) & (kj <= qi + BAND)
    first_mask = (lax.broadcasted_iota(jnp.int32, (n2, BAND), 1)
                  <= lax.broadcasted_iota(jnp.int32, (n2, BAND), 0) % BAND)

    order = sorted(range(N_GROUPS), key=lambda g: -DIL_GROUPS[g][1])
    assert DIL_GROUPS[order[-1]][1] == 1
    slot = {g: i for i, g in enumerate(order[:-1])}
    for g in order:
        win, dil = DIL_GROUPS[g]
        assert win // dil == BAND
        for r in range(dil):
            for blk in range(L // dil // BAND):
                q_rows = pl.ds(r + dil * BAND * blk, BAND, stride=dil)
                q = jnp.concatenate([q_refs[g * Q_PER_KV + j][0, q_rows, :]
                                     for j in range(Q_PER_KV)], axis=0).astype(BF16)
                if blk == 0:
                    k_rows, mask = q_rows, first_mask
                else:
                    k_rows = pl.ds(r + dil * BAND * (blk - 1), 2 * BAND, stride=dil)
                    mask = band_mask
                kb = k_ref[0, k_rows, :].astype(BF16)
                vb = v_ref[0, k_rows, :].astype(BF16)
                s = jnp.where(mask, _dot_nt(q, kb) * scale, NEG_INF)
                for j in range(Q_PER_KV):
                    s_j = s[j * BAND:(j + 1) * BAND]
                    m = jnp.max(s_j, axis=-1, keepdims=True)
                    p = jnp.exp(s_j - m)
                    l = jnp.sum(p, axis=-1, keepdims=True)
                    out = _dot(p.astype(BF16), vb) / l
                    lse = jnp.broadcast_to(m + jnp.log(l), (BAND, LANES))
                    if dil > 1:
                        og_ref[slot[g], j, q_rows, :] = out
                        lse_ref[slot[g], j, q_rows, :] = lse
                    else:
                        rows = slice(blk * BAND, (blk + 1) * BAND)
                        others = [(og_ref[i, j, rows, :], lse_ref[i, j, rows, :]) for i in slot.values()]
                        top = functools.reduce(jnp.maximum, [ls for _, ls in others], lse)
                        w = jnp.exp(lse - top)
                        num, den = w * out, w
                        for o_e, ls_e in others:
                            w = jnp.exp(ls_e - top)
                            num, den = num + w * o_e, den + w
                        o_ref[0, rows, j * HEAD_DIM:(j + 1) * HEAD_DIM] = (num / den).astype(o_ref.dtype)


def _attn_seq(q, k, v):
    b, l, _ = q.shape
    pair = Q_PER_KV * HEAD_DIM
    q_spec = lambda g, j: pl.BlockSpec((1, l, HEAD_DIM), lambda i, h: (i, 0, (g * N_KV + h) * Q_PER_KV + j))
    q_specs = [q_spec(g, j) for g in range(N_GROUPS) for j in range(Q_PER_KV)]
    kv_spec = pl.BlockSpec((1, l, HEAD_DIM), lambda i, h: (i, 0, h))
    stat = pltpu.VMEM((N_GROUPS - 1, Q_PER_KV, l, LANES), F32)
    return pl.pallas_call(
        functools.partial(_attn_seq_kernel, L=l),
        grid=(b, N_KV),
        in_specs=q_specs + [kv_spec, kv_spec],
        out_specs=pl.BlockSpec((1, l, pair), lambda i, h: (i, 0, h)),
        out_shape=jax.ShapeDtypeStruct((b, l, N_QH_GROUP * HEAD_DIM), BF16),
        scratch_shapes=[stat, stat],
        compiler_params=_params("parallel", "arbitrary"),
        name="attn_seq",
    )(*([q] * len(q_specs)), k, v)


def _step_key_rows(dil, t):
    return BAND * (N_KV if dil == 1 else t * N_KV) + BAND


def _step_bias(t):
    slot, ti = np.divmod(np.arange(N_QH_GROUP * t), t)
    kvh = (slot // Q_PER_KV)[:, None]
    ti = ti[:, None]
    new = np.arange(BAND)[None, :]
    new_t, new_h, is_new = new // N_KV, new % N_KV, new < t * N_KV
    parts = []
    for win, dil in DIL_GROUPS:
        rows = np.arange(_step_key_rows(dil, t) - BAND)[None, :]
        if dil == 1:
            ok = (rows % N_KV == kvh) & (rows // N_KV >= ti)
            ok_new = is_new & (new_h == kvh) & (new_t <= ti)
        else:
            ok = rows % (t * N_KV) == ti * N_KV + kvh
            ok_new = is_new & (new_h == kvh) & (new_t == ti)
        parts += [ok, ok_new]
    return jnp.asarray(np.where(np.concatenate(parts, axis=1), 0.0, NEG_INF), F32)


def _attn_step_kernel(*refs, T, nb, sources):
    q_ref, kn_ref, vn_ref = refs[:3]
    bias_ref, o_ref = refs[-2:]
    cached = refs[3:-2]
    scale = HEAD_DIM ** -0.5
    pad = jnp.zeros((BAND - T * N_KV, HEAD_DIM), BF16)
    for bj in range(nb):
        tok = slice(bj * T, (bj + 1) * T)
        new = slice(bj * T * N_KV, (bj + 1) * T * N_KV)
        k_new = jnp.concatenate([kn_ref[new, :].astype(BF16), pad], axis=0)
        v_new = jnp.concatenate([vn_ref[new, :].astype(BF16), pad], axis=0)
        blocks = [(ref[bj].reshape(-1, HEAD_DIM).astype(BF16)) for ref in cached]
        scores, values = [], []
        start = 0
        for g, (pair, first, count) in enumerate(sources):
            keys = jnp.concatenate([blocks[2 * pair][first:first + count], k_new], axis=0)
            values.append(jnp.concatenate([blocks[2 * pair + 1][first:first + count], v_new], axis=0))
            q = jnp.concatenate([q_ref[tok, ((g * N_KV + h) * Q_PER_KV + j) * HEAD_DIM:
                                       ((g * N_KV + h) * Q_PER_KV + j + 1) * HEAD_DIM]
                                 for h in range(N_KV) for j in range(Q_PER_KV)], axis=0)
            n = count + BAND
            scores.append(_dot_nt(q.astype(BF16), keys) * scale + bias_ref[:, start:start + n])
            start += n
        m = functools.reduce(jnp.maximum, [jnp.max(s, axis=-1, keepdims=True) for s in scores])
        probs = [jnp.exp(s - m) for s in scores]
        l = functools.reduce(jnp.add, [jnp.sum(p, axis=-1, keepdims=True) for p in probs])
        acc = functools.reduce(jnp.add, [_dot(p.astype(BF16), v) for p, v in zip(probs, values)])
        out = acc / l
        for slot in range(N_QH_GROUP):
            o_ref[tok, slot * HEAD_DIM:(slot + 1) * HEAD_DIM] = out[slot * T:(slot + 1) * T].astype(o_ref.dtype)


def _attn_step(q, k_new, v_new, cache_k, cache_v, t, nb):
    m = q.shape[0]
    b = m // t
    past = cache_k.shape[1]
    assert all(win // dil == BAND and past >= win for win, dil in DIL_GROUPS)
    assert t <= min(d for _, d in DIL_GROUPS if d > 1) and t <= BAND
    views, specs, sources = [], [], {}
    for g, (win, dil) in enumerate(DIL_GROUPS):
        if dil == 1:
            continue
        shape = (b, past // dil, dil * N_KV, HEAD_DIM)
        assert (past - win) % (dil * BAND) == 0
        blk = (past - win) // (dil * BAND)
        spec = pl.BlockSpec((nb, BAND, t * N_KV, HEAD_DIM), lambda i, blk=blk: (i, blk, 0, 0))
        sources[g] = (len(views) // 2, 0, BAND * t * N_KV)
        if dil == t:
            contiguous = (len(views) // 2, win)
        views += [cache_k.reshape(shape), cache_v.reshape(shape)]
        specs += [spec, spec]
    for g, (win, dil) in enumerate(DIL_GROUPS):
        if dil == 1:
            pair, covered = contiguous
            assert covered >= BAND
            sources[g] = (pair, (covered - BAND) * N_KV, BAND * N_KV)
    tok = lambda width: pl.BlockSpec((nb * t, width), lambda i: (i, 0))
    new = pl.BlockSpec((nb * t * N_KV, HEAD_DIM), lambda i: (i, 0))
    bias = _step_bias(t)
    return pl.pallas_call(
        functools.partial(_attn_step_kernel, T=t, nb=nb, sources=tuple(sources[g] for g in range(N_GROUPS))),
        grid=(b // nb,),
        in_specs=[tok(q.shape[1]), new, new] + specs + [pl.BlockSpec(bias.shape, lambda i: (0, 0))],
        out_specs=tok(N_QH_GROUP * HEAD_DIM),
        out_shape=jax.ShapeDtypeStruct((m, N_QH_GROUP * HEAD_DIM), BF16),
        compiler_params=_params("parallel"),
        name="attn_step",
    )(q, k_new.reshape(m * N_KV, HEAD_DIM), v_new.reshape(m * N_KV, HEAD_DIM), *views, bias)


FFN_SPLITS = 2


def _post_mixer_kernel(x_ref, o_ref, p_ref, wo_ref, fg_ref, fin_ref, fout_ref, eg_ref, eproj_ref, egate_ref, y_ref):
    x = x_ref[...] + _dot(o_ref[...], wo_ref[...])
    h = _rms(x, fg_ref[...]).astype(BF16)
    width = FFN_HIDDEN // FFN_SPLITS
    ffn = None
    for c in range(FFN_SPLITS):
        a = _dot(h, fin_ref[:, c * width:(c + 1) * width])
        b = _dot(h, fin_ref[:, FFN_HIDDEN + c * width:FFN_HIDDEN + (c + 1) * width])
        part = _dot((a * jax.nn.sigmoid(a) * b).astype(BF16), fout_ref[c * width:(c + 1) * width, :])
        ffn = part if ffn is None else ffn + part
    x = x + ffn
    h = _rms(x, eg_ref[...]).astype(BF16)
    gate = jax.nn.sigmoid(_dot(h, egate_ref[...]))
    y_ref[...] = x + _dot(p_ref[...].astype(BF16), eproj_ref[...]) * gate


def _post_mixer(x, o, p, wo, fg, fin, fout, eg, eproj, egate, tm):
    m = x.shape[0]
    row = lambda width: pl.BlockSpec((tm, width), lambda i: (i, 0))
    p_stack, layer = p
    return pl.pallas_call(
        _post_mixer_kernel,
        grid=(m // tm,),
        in_specs=[row(D_MODEL), row(o.shape[1]), pl.BlockSpec((None, tm, PLE_DIM), lambda i: (layer, i, 0)),
                  _wspec(wo), _wspec(fg), _wspec(fin), _wspec(fout), _wspec(eg), _wspec(eproj), _wspec(egate)],
        out_specs=row(D_MODEL),
        out_shape=jax.ShapeDtypeStruct((m, D_MODEL), F32),
        compiler_params=_params("parallel"),
        name="post_mixer",
    )(x, o, p_stack, wo[0], fg[0], fin[0], fout[0], eg[0], eproj[0], egate[0])


PROJ_ROWS = 256
POST_ROWS = 512


def _trunk(x, p, gla_s0, cache, pos, w):
    b, l, _ = x.shape
    m = b * l
    depth = p.shape[0]
    n_a = depth // 2
    step = gla_s0 is not None
    xf = x.reshape(m, D_MODEL)
    pf = p.reshape(depth, m, PLE_DIM)
    tm = min(PROJ_ROWS, m)
    tm_post = min(POST_ROWS, m)
    rope = _rope_tables(pos)
    if step:
        rope = jnp.tile(rope, (1, tm // l, 1))
    lw = lambda name, layer: (w[name], layer)
    states = []
    k_new = v_new = None
    for i in range(depth):
        if i < n_a:
            qk, v, r, glr = _proj_a(xf, lw("a_norm", i), lw("a_w_in", i), F32 if step else BF16, tm)
            gate_w = (lw("a_w_gk_up", i), lw("a_b_gk", i), lw("a_onorm", i))
            if step:
                o, s = _gla_step(qk, v, r, glr, *gate_w, (gla_s0, i), l, 8 // l)
            else:
                o, s = _gla_seq(qk.reshape(b, l, -1), v.reshape(b, l, -1), r.reshape(b, l, -1), glr.reshape(b, l, -1),
                                *gate_w, T=256)
                o = o.reshape(m, GLA_V)
            states.append(s)
            wo = lw("a_w_o", i)
        else:
            j = i - n_a
            kv_w = N_KV * HEAD_DIM
            if j == 0:
                by_head = [(0, N_KV, True), (N_KV, N_KV, True)]
                flat = [] if step else [(0, N_KV, False), (N_KV, N_KV, False)]
                kv = _head_proj(xf, lw("kv_norm", 0), lw("w_kv", 0), lw("k_norm", 0), rope, N_KV,
                                by_head + flat, (F32,) * (2 + len(flat)), tm)
                k_new, v_new = kv[:2]
            n_q = N_GROUPS * N_QH_GROUP
            (q,) = _head_proj(xf, lw("b_norm", j), lw("b_w_q", j), lw("b_q_norm", j), rope, n_q,
                              [(0, n_q, False)], (F32,), tm)
            if step:
                o = _attn_step(q, k_new, v_new, *cache, l, 8 // l)
            else:
                o = _attn_seq(q.reshape(b, l, -1), kv[2].reshape(b, l, kv_w), kv[3].reshape(b, l, kv_w))
                o = o.reshape(m, N_QH_GROUP * HEAD_DIM)
            wo = lw("b_w_o", j)
        xf = _post_mixer(xf, o, (pf, i), wo, lw("f_norm", i), lw("f_w_in", i), lw("f_w_out", i),
                         lw("e_norm", i), lw("e_w_proj", i), lw("e_w_gate", i), tm_post)
    y = xf.reshape(b, l, D_MODEL)
    return (y, jnp.stack(states), k_new.reshape(b, l, N_KV, HEAD_DIM), v_new.reshape(b, l, N_KV, HEAD_DIM))


def kernel(x_prompt, x_sample, p_prompt, p_sample, state_gla, cache_k, cache_v, a_norm, a_w_in, a_w_gk_up, a_b_gk,
           a_onorm, a_w_o, kv_norm, w_kv, k_norm, b_norm, b_w_q, b_q_norm, b_w_o, f_norm, f_w_in, f_w_out, e_norm,
           e_w_proj, e_w_gate):
    n_a = a_norm.shape[0]
    depth = f_norm.shape[0]
    main = 2 * GLA_QK + 2 * GLA_V
    a_w_in_p = jnp.concatenate(
        [a_w_in[:, :, :main], jnp.pad(a_w_in[:, :, main:], ((0, 0), (0, 0), (0, GLR_PAD - GLA_LR)))], axis=2)
    w = dict(
        a_norm=a_norm.reshape(n_a, 1, D_MODEL),
        a_w_in=a_w_in_p.astype(BF16),
        a_w_gk_up=jnp.pad(a_w_gk_up, ((0, 0), (0, GLR_PAD - GLA_LR), (0, 0))).astype(BF16),
        a_b_gk=a_b_gk.reshape(n_a, 1, GLA_QK),
        a_onorm=a_onorm.reshape(n_a, 1, GLA_DV),
        a_w_o=a_w_o.astype(BF16),
        kv_norm=kv_norm.reshape(1, 1, D_MODEL),
        w_kv=w_kv.astype(BF16)[None],
        k_norm=k_norm.reshape(1, 1, HEAD_DIM),
        b_norm=b_norm.reshape(-1, 1, D_MODEL),
        b_w_q=b_w_q.astype(BF16),
        b_q_norm=b_q_norm.reshape(-1, 1, HEAD_DIM),
        b_w_o=b_w_o.astype(BF16),
        f_norm=f_norm.reshape(depth, 1, D_MODEL),
        f_w_in=f_w_in.astype(BF16),
        f_w_out=f_w_out.astype(BF16),
        e_norm=e_norm.reshape(depth, 1, D_MODEL),
        e_w_proj=e_w_proj.astype(BF16),
        e_w_gate=e_w_gate.astype(BF16),
    )
    seq = x_prompt.shape[1]
    dec_seq = x_sample.shape[1]
    pos_prompt = jnp.arange(seq, dtype=F32)
    pos_sample = PAST_LEN + jnp.arange(dec_seq, dtype=F32)
    y_p, gs_p, k_p, v_p = _trunk(x_prompt, p_prompt, None, None, pos_prompt, w)
    y_s, gs_s, k_s, v_s = _trunk(x_sample, p_sample, state_gla, (cache_k, cache_v), pos_sample, w)
    keep = min(MAX_WINDOW, seq)
    return (y_p, y_s, gs_p, gs_s, k_p[:, seq - keep:], v_p[:, seq - keep:], k_s, v_s)
```

```python
import functools
import math

import jax
import jax.numpy as jnp
import numpy as np
from jax import lax
from jax.experimental import pallas as pl
from jax.experimental.pallas import tpu as pltpu

D_MODEL = 1024
PAST_LEN = 2048
GLA_HEADS = 4
GLA_DK = 128
GLA_DV = 256
GLA_QK = GLA_HEADS * GLA_DK
GLA_V = GLA_HEADS * GLA_DV
GLA_LR = 16
GLA_TAU = 16.0
GLA_CHUNK = 64
HEAD_DIM = 128
N_KV = 4
Q_PER_KV = 2
N_QH_GROUP = N_KV * Q_PER_KV
DIL_GROUPS = ((128, 1), (512, 4), (2048, 16))
N_GROUPS = len(DIL_GROUPS)
MAX_WINDOW = 2048
ROPE_DIM = HEAD_DIM // 4
ROPE_HALF = ROPE_DIM // 2
ROPE_THETA = 500000.0
FFN_HIDDEN = 2816
PLE_DIM = 256
EPS = 1e-6
NEG_INF = -1e30

LANES = 128
GLR_PAD = LANES
VMEM_LIMIT = 56 * 1024 * 1024

F32 = jnp.float32
BF16 = jnp.bfloat16


def _params(*sem):
    return pltpu.CompilerParams(dimension_semantics=sem, vmem_limit_bytes=VMEM_LIMIT)


def _wspec(lw):
    stack, layer = lw
    index = (layer,) + (0,) * (stack.ndim - 1)
    return pl.BlockSpec((None,) + stack.shape[1:], lambda *_: index, pipeline_mode=pl.Buffered(1))


def _rms(x, g):
    var = jnp.mean(x * x, axis=-1, keepdims=True)
    return x * lax.rsqrt(var + EPS) * g


def _log_sigmoid(x):
    return jnp.minimum(x, 0.0) - jnp.log1p(jnp.exp(-jnp.abs(x)))


def _dot(a, b):
    return jnp.dot(a, b, preferred_element_type=F32)


def _dot_nt(a, b):
    return lax.dot_general(a, b, (((1,), (1,)), ((), ())), preferred_element_type=F32)


def _dot_tn(a, b):
    return lax.dot_general(a, b, (((0,), (0,)), ((), ())), preferred_element_type=F32)


def _proj_a_kernel(x_ref, g_ref, w_ref, qk_ref, v_ref, r_ref, glr_ref):
    h = _rms(x_ref[...], g_ref[...]).astype(BF16)
    qk_ref[...] = _dot(h, w_ref[:, 0:2 * GLA_QK])
    v_ref[...] = _dot(h, w_ref[:, 2 * GLA_QK:2 * GLA_QK + GLA_V]).astype(BF16)
    r_ref[...] = _dot(h, w_ref[:, 2 * GLA_QK + GLA_V:2 * GLA_QK + 2 * GLA_V])
    glr_ref[...] = _dot(h, w_ref[:, 2 * GLA_QK + 2 * GLA_V:]).astype(BF16)


def _proj_a(x, g, w, tm):
    m = x.shape[0]
    row = lambda width: pl.BlockSpec((tm, width), lambda i: (i, 0))
    return pl.pallas_call(
        _proj_a_kernel,
        grid=(m // tm,),
        in_specs=[row(D_MODEL), _wspec(g), _wspec(w)],
        out_specs=[row(2 * GLA_QK), row(GLA_V), row(GLA_V), row(GLR_PAD)],
        out_shape=[jax.ShapeDtypeStruct((m, 2 * GLA_QK), F32),
                   jax.ShapeDtypeStruct((m, GLA_V), BF16),
                   jax.ShapeDtypeStruct((m, GLA_V), F32),
                   jax.ShapeDtypeStruct((m, GLR_PAD), BF16)],
        compiler_params=_params("parallel"),
        name="proj_a",
    )(x, g[0], w[0])


def _split3(x):
    hi = x.astype(BF16)
    r1 = x - hi.astype(F32)
    mid = r1.astype(BF16)
    lo = (r1 - mid.astype(F32)).astype(BF16)
    return hi, mid, lo


def _gla_chunk_terms(glr, wgk, bgk, qk, C):
    T = qk.shape[0]
    log_a = _log_sigmoid(_dot(glr, wgk) + bgk) * (1.0 / GLA_TAU)
    row = lax.broadcasted_iota(jnp.int32, (T, T), 0)
    col = lax.broadcasted_iota(jnp.int32, (T, T), 1)
    same_chunk = (row // C) == (col // C)
    causal = same_chunk & (col <= row)
    tril = jnp.where(causal, 1.0, 0.0).astype(BF16)
    chunk_ones = jnp.where(same_chunk, 1.0, 0.0).astype(BF16)
    parts = jnp.concatenate(_split3(log_a), axis=1)
    fold = lambda y: y[:, :GLA_QK] + y[:, GLA_QK:2 * GLA_QK] + y[:, 2 * GLA_QK:]
    cum = fold(_dot(tril, parts))
    total = fold(_dot(chunk_ones, parts))
    q = qk[:, :GLA_QK] * (GLA_DK ** -0.5)
    k = qk[:, GLA_QK:]
    return q * jnp.exp(cum), (k * jnp.exp(-cum)).astype(BF16), k * jnp.exp(total - cum), total, causal


def _state_decay(total_row):
    decay = jnp.exp(jnp.broadcast_to(total_row, (GLA_DK, GLA_DK)).T)
    return jnp.concatenate([decay] * (GLA_DV // GLA_DK), axis=1)


def _gla_gated(o, onorm, gate):
    return _rms(o, onorm) * (gate * jax.nn.sigmoid(gate))


def _gla_seq_kernel(qk_ref, v_ref, r_ref, glr_ref, wgk_ref, bgk_ref, on_ref, o_ref, s_ref, *, T, C):
    @pl.when(pl.program_id(1) == 0)
    def _():
        s_ref[...] = jnp.zeros_like(s_ref)

    q_in, k_in, k_end, total, causal = _gla_chunk_terms(glr_ref[0], wgk_ref[...], bgk_ref[...], qk_ref[0], C)
    q_in, k_end = q_in.astype(BF16), k_end.astype(BF16)
    onorm = on_ref[...]
    for h in range(GLA_HEADS):
        lanes = slice(h * GLA_DK, (h + 1) * GLA_DK)
        v = v_ref[0, :, h * GLA_DV:(h + 1) * GLA_DV]
        att = jnp.where(causal, _dot_nt(q_in[:, lanes], k_in[:, lanes]), 0.0).astype(BF16)
        o_intra = _dot(att, v)
        state = s_ref[0, h]
        o_inter = []
        for c in range(T // C):
            rows = slice(c * C, (c + 1) * C)
            o_inter.append(_dot(q_in[rows, lanes], state.astype(BF16)))
            state = state * _state_decay(total[c * C:c * C + 1, lanes]) + _dot_tn(k_end[rows, lanes], v[rows])
        s_ref[0, h] = state
        o = o_intra + jnp.concatenate(o_inter, axis=0)
        og = _gla_gated(o, onorm, r_ref[0, :, h * GLA_DV:(h + 1) * GLA_DV])
        o_ref[0, :, h * GLA_DV:(h + 1) * GLA_DV] = og.astype(o_ref.dtype)


def _gla_seq(qk, v, r, glr, wgk, bgk, onorm, T):
    b, l, _ = qk.shape
    tok = lambda width: pl.BlockSpec((1, T, width), lambda i, t: (i, t, 0))
    return pl.pallas_call(
        functools.partial(_gla_seq_kernel, T=T, C=GLA_CHUNK),
        grid=(b, l // T),
        in_specs=[tok(2 * GLA_QK), tok(GLA_V), tok(GLA_V), tok(GLR_PAD),
                  _wspec(wgk), _wspec(bgk), _wspec(onorm)],
        out_specs=[tok(GLA_V),
                   pl.BlockSpec((1, GLA_HEADS, GLA_DK, GLA_DV), lambda i, t: (i, 0, 0, 0))],
        out_shape=[jax.ShapeDtypeStruct((b, l, GLA_V), BF16),
                   jax.ShapeDtypeStruct((b, GLA_HEADS, GLA_DK, GLA_DV), F32)],
        compiler_params=_params("parallel", "arbitrary"),
        name="gla_seq",
    )(qk, v, r, glr, wgk[0], bgk[0], onorm[0])


def _gla_step_kernel(qk_ref, v_ref, r_ref, glr_ref, wgk_ref, bgk_ref, on_ref, s0_ref, o_ref, s_ref, *, nb, L):
    q_in, k_in, k_end, total, causal = _gla_chunk_terms(glr_ref[...], wgk_ref[...], bgk_ref[...], qk_ref[...], L)
    chunk = lax.broadcasted_iota(jnp.int32, (nb * L, 1), 0) // L
    onorm = on_ref[...]
    for h in range(GLA_HEADS):
        lanes = slice(h * GLA_DK, (h + 1) * GLA_DK)
        v = v_ref[:, h * GLA_DV:(h + 1) * GLA_DV]
        att = jnp.where(causal, _dot_nt(q_in[:, lanes].astype(BF16), k_in[:, lanes]), 0.0).astype(BF16)
        o = _dot(att, v)
        for j in range(nb):
            mine = chunk == j
            state = s0_ref[j, h]
            o = o + _dot(jnp.where(mine, q_in[:, lanes], 0.0).astype(BF16), state.astype(BF16))
            s_ref[j, h] = (state * _state_decay(total[j * L:j * L + 1, lanes])
                           + _dot_tn(jnp.where(mine, k_end[:, lanes], 0.0).astype(BF16), v))
        og = _gla_gated(o, onorm, r_ref[:, h * GLA_DV:(h + 1) * GLA_DV])
        o_ref[:, h * GLA_DV:(h + 1) * GLA_DV] = og.astype(o_ref.dtype)


def _gla_step(qk, v, r, glr, wgk, bgk, onorm, s0, L, nb):
    m = qk.shape[0]
    b = m // L
    rows = nb * L
    tok = lambda width: pl.BlockSpec((rows, width), lambda i: (i, 0))
    st = pl.BlockSpec((nb, GLA_HEADS, GLA_DK, GLA_DV), lambda i: (i, 0, 0, 0))
    s0_stack, layer = s0
    st_in = pl.BlockSpec((None, nb, GLA_HEADS, GLA_DK, GLA_DV), lambda i: (layer, i, 0, 0, 0))
    return pl.pallas_call(
        functools.partial(_gla_step_kernel, nb=nb, L=L),
        grid=(b // nb,),
        in_specs=[tok(2 * GLA_QK), tok(GLA_V), tok(GLA_V), tok(GLR_PAD),
                  _wspec(wgk), _wspec(bgk), _wspec(onorm), st_in],
        out_specs=[tok(GLA_V), st],
        out_shape=[jax.ShapeDtypeStruct((m, GLA_V), BF16),
                   jax.ShapeDtypeStruct((b, GLA_HEADS, GLA_DK, GLA_DV), F32)],
        compiler_params=_params("parallel"),
        name="gla_step",
    )(qk, v, r, glr, wgk[0], bgk[0], onorm[0], s0_stack)


HEADS_PER_DOT = 8


def _head_proj_kernel(x_ref, g_ref, w_ref, hg_ref, rope_ref, *out_refs, n_rot, outs):
    h = _rms(x_ref[...], g_ref[...]).astype(BF16)
    hg = hg_ref[...]
    cos, sin_lo, sin_hi = rope_ref[0], rope_ref[1], rope_ref[2]
    ones = jnp.ones((HEAD_DIM, HEAD_DIM), BF16)
    n_heads = w_ref.shape[1] // HEAD_DIM
    for start in range(0, n_heads, HEADS_PER_DOT):
        stop = min(start + HEADS_PER_DOT, n_heads)
        y = _dot(h, w_ref[:, start * HEAD_DIM:stop * HEAD_DIM])
        for head in range(start, stop):
            yh = y[:, (head - start) * HEAD_DIM:(head - start + 1) * HEAD_DIM]
            if head < n_rot:
                ss = _dot((yh * yh).astype(BF16), ones)
                yn = yh * lax.rsqrt(ss * (1.0 / HEAD_DIM) + EPS) * hg
                yh = (yn * cos + pltpu.roll(yn, HEAD_DIM - ROPE_HALF, axis=1) * sin_lo
                      + pltpu.roll(yn, ROPE_HALF, axis=1) * sin_hi)
            for ref, (first, count, by_head) in zip(out_refs, outs):
                i = head - first
                if 0 <= i < count:
                    if by_head:
                        ref[:, i, :] = yh.astype(ref.dtype)
                    else:
                        ref[:, i * HEAD_DIM:(i + 1) * HEAD_DIM] = yh.astype(ref.dtype)


def _head_proj(x, g, w, hg, rope, n_rot, outs, dtypes, tm):
    m = x.shape[0]
    n_rope_tiles = rope.shape[1] // tm
    row = lambda width: pl.BlockSpec((tm, width), lambda i: (i, 0))
    out_specs, out_shape = [], []
    for (first, count, by_head), dt in zip(outs, dtypes):
        if by_head:
            out_specs.append(pl.BlockSpec((tm, count, HEAD_DIM), lambda i: (i, 0, 0)))
            out_shape.append(jax.ShapeDtypeStruct((m, count, HEAD_DIM), dt))
        else:
            out_specs.append(row(count * HEAD_DIM))
            out_shape.append(jax.ShapeDtypeStruct((m, count * HEAD_DIM), dt))
    return pl.pallas_call(
        functools.partial(_head_proj_kernel, n_rot=n_rot, outs=outs),
        grid=(m // tm,),
        in_specs=[row(D_MODEL), _wspec(g), _wspec(w), _wspec(hg),
                  pl.BlockSpec((3, tm, HEAD_DIM), lambda i: (0, i % n_rope_tiles, 0))],
        out_specs=out_specs,
        out_shape=out_shape,
        compiler_params=_params("parallel"),
        name="head_proj",
    )(x, g[0], w[0], hg[0], rope)


def _rope_tables(pos):
    freqs = jnp.power(ROPE_THETA, -jnp.arange(ROPE_HALF, dtype=F32) * 2.0 / ROPE_DIM)
    ang = pos[:, None] * freqs[None, :]
    cos, sin = jnp.cos(ang), jnp.sin(ang)
    n = pos.shape[0]
    rest = HEAD_DIM - ROPE_DIM
    c = jnp.concatenate([cos, cos, jnp.ones((n, rest), F32)], axis=1)
    s_lo = jnp.concatenate([-sin, jnp.zeros((n, ROPE_HALF + rest), F32)], axis=1)
    s_hi = jnp.concatenate([jnp.zeros((n, ROPE_HALF), F32), sin, jnp.zeros((n, rest), F32)], axis=1)
    return jnp.stack([c, s_lo, s_hi])


BAND = 128


def _attn_seq_kernel(*refs, L):
    n_q = N_GROUPS * Q_PER_KV
    q_refs = refs[:n_q]
    k_ref, v_ref, o_ref, og_ref, lse_ref = refs[n_q:]
    scale = HEAD_DIM ** -0.5
    n2 = Q_PER_KV * BAND
    qi = lax.broadcasted_iota(jnp.int32, (n2, 2 * BAND), 0) % BAND
    kj = lax.broadcasted_iota(jnp.int32, (n2, 2 * BAND), 1)
    band_mask = (kj >= qi) & (kj <= qi + BAND)
    first_mask = (lax.broadcasted_iota(jnp.int32, (n2, BAND), 1)
                  <= lax.broadcasted_iota(jnp.int32, (n2, BAND), 0) % BAND)

    order = sorted(range(N_GROUPS), key=lambda g: -DIL_GROUPS[g][1])
    assert DIL_GROUPS[order[-1]][1] == 1
    slot = {g: i for i, g in enumerate(order[:-1])}
    for g in order:
        win, dil = DIL_GROUPS[g]
        assert win // dil == BAND
        for r in range(dil):
            for blk in range(L // dil // BAND):
                q_rows = pl.ds(r + dil * BAND * blk, BAND, stride=dil)
                q = jnp.concatenate([q_refs[g * Q_PER_KV + j][0, q_rows, :]
                                     for j in range(Q_PER_KV)], axis=0).astype(BF16)
                if blk == 0:
                    k_rows, mask = q_rows, first_mask
                else:
                    k_rows = pl.ds(r + dil * BAND * (blk - 1), 2 * BAND, stride=dil)
                    mask = band_mask
                kb = k_ref[0, k_rows, :].astype(BF16)
                vb = v_ref[0, k_rows, :].astype(BF16)
                s = jnp.where(mask, _dot_nt(q, kb) * scale, NEG_INF)
                for j in range(Q_PER_KV):
                    s_j = s[j * BAND:(j + 1) * BAND]
                    m = jnp.max(s_j, axis=-1, keepdims=True)
                    p = jnp.exp(s_j - m)
                    l = jnp.sum(p, axis=-1, keepdims=True)
                    out = _dot(p.astype(BF16), vb) / l
                    lse = jnp.broadcast_to(m + jnp.log(l), (BAND, LANES))
                    if dil > 1:
                        og_ref[slot[g], j, q_rows, :] = out
                        lse_ref[slot[g], j, q_rows, :] = lse
                    else:
                        rows = slice(blk * BAND, (blk + 1) * BAND)
                        others = [(og_ref[i, j, rows, :], lse_ref[i, j, rows, :]) for i in slot.values()]
                        top = functools.reduce(jnp.maximum, [ls for _, ls in others], lse)
                        w = jnp.exp(lse - top)
                        num, den = w * out, w
                        for o_e, ls_e in others:
                            w = jnp.exp(ls_e - top)
                            num, den = num + w * o_e, den + w
                        o_ref[0, rows, j * HEAD_DIM:(j + 1) * HEAD_DIM] = (num / den).astype(o_ref.dtype)


def _attn_seq(q, k, v):
    b, l, _ = q.shape
    pair = Q_PER_KV * HEAD_DIM
    q_spec = lambda g, j: pl.BlockSpec((1, l, HEAD_DIM), lambda i, h: (i, 0, (g * N_KV + h) * Q_PER_KV + j))
    q_specs = [q_spec(g, j) for g in range(N_GROUPS) for j in range(Q_PER_KV)]
    kv_spec = pl.BlockSpec((1, l, HEAD_DIM), lambda i, h: (i, 0, h))
    stat = pltpu.VMEM((N_GROUPS - 1, Q_PER_KV, l, LANES), F32)
    return pl.pallas_call(
        functools.partial(_attn_seq_kernel, L=l),
        grid=(b, N_KV),
        in_specs=q_specs + [kv_spec, kv_spec],
        out_specs=pl.BlockSpec((1, l, pair), lambda i, h: (i, 0, h)),
        out_shape=jax.ShapeDtypeStruct((b, l, N_QH_GROUP * HEAD_DIM), BF16),
        scratch_shapes=[stat, stat],
        compiler_params=_params("parallel", "arbitrary"),
        name="attn_seq",
    )(*([q] * len(q_specs)), k, v)


def _step_key_rows(dil, t):
    return BAND * (N_KV if dil == 1 else t * N_KV) + BAND


def _step_bias(t):
    slot, ti = np.divmod(np.arange(N_QH_GROUP * t), t)
    kvh = (slot // Q_PER_KV)[:, None]
    ti = ti[:, None]
    new = np.arange(BAND)[None, :]
    new_t, new_h, is_new = new // N_KV, new % N_KV, new < t * N_KV
    parts = []
    for win, dil in DIL_GROUPS:
        rows = np.arange(_step_key_rows(dil, t) - BAND)[None, :]
        if dil == 1:
            ok = (rows % N_KV == kvh) & (rows // N_KV >= ti)
            ok_new = is_new & (new_h == kvh) & (new_t <= ti)
        else:
            ok = rows % (t * N_KV) == ti * N_KV + kvh
            ok_new = is_new & (new_h == kvh) & (new_t == ti)
        parts += [ok, ok_new]
    return jnp.asarray(np.where(np.concatenate(parts, axis=1), 0.0, NEG_INF), F32)


def _attn_step_kernel(*refs, T, nb, sources):
    q_ref, kn_ref, vn_ref = refs[:3]
    bias_ref, o_ref = refs[-2:]
    cached = refs[3:-2]
    scale = HEAD_DIM ** -0.5
    pad = jnp.zeros((BAND - T * N_KV, HEAD_DIM), BF16)
    for bj in range(nb):
        tok = slice(bj * T, (bj + 1) * T)
        new = slice(bj * T * N_KV, (bj + 1) * T * N_KV)
        k_new = jnp.concatenate([kn_ref[new, :].astype(BF16), pad], axis=0)
        v_new = jnp.concatenate([vn_ref[new, :].astype(BF16), pad], axis=0)
        blocks = [(ref[bj].reshape(-1, HEAD_DIM).astype(BF16)) for ref in cached]
        scores, values = [], []
        start = 0
        for g, (pair, first, count) in enumerate(sources):
            keys = jnp.concatenate([blocks[2 * pair][first:first + count], k_new], axis=0)
            values.append(jnp.concatenate([blocks[2 * pair + 1][first:first + count], v_new], axis=0))
            q = jnp.concatenate([q_ref[tok, ((g * N_KV + h) * Q_PER_KV + j) * HEAD_DIM:
                                       ((g * N_KV + h) * Q_PER_KV + j + 1) * HEAD_DIM]
                                 for h in range(N_KV) for j in range(Q_PER_KV)], axis=0)
            n = count + BAND
            scores.append(_dot_nt(q.astype(BF16), keys) * scale + bias_ref[:, start:start + n])
            start += n
        m = functools.reduce(jnp.maximum, [jnp.max(s, axis=-1, keepdims=True) for s in scores])
        probs = [jnp.exp(s - m) for s in scores]
        l = functools.reduce(jnp.add, [jnp.sum(p, axis=-1, keepdims=True) for p in probs])
        acc = functools.reduce(jnp.add, [_dot(p.astype(BF16), v) for p, v in zip(probs, values)])
        out = acc / l
        for slot in range(N_QH_GROUP):
            o_ref[tok, slot * HEAD_DIM:(slot + 1) * HEAD_DIM] = out[slot * T:(slot + 1) * T].astype(o_ref.dtype)


def _attn_step(q, k_new, v_new, cache_k, cache_v, t, nb):
    m = q.shape[0]
    b = m // t
    past = cache_k.shape[1]
    assert all(win // dil == BAND and past >= win for win, dil in DIL_GROUPS)
    assert t <= min(d for _, d in DIL_GROUPS if d > 1) and t <= BAND
    views, specs, sources = [], [], {}
    for g, (win, dil) in enumerate(DIL_GROUPS):
        if dil == 1:
            continue
        shape = (b, past // dil, dil * N_KV, HEAD_DIM)
        assert (past - win) % (dil * BAND) == 0
        blk = (past - win) // (dil * BAND)
        spec = pl.BlockSpec((nb, BAND, t * N_KV, HEAD_DIM), lambda i, blk=blk: (i, blk, 0, 0))
        sources[g] = (len(views) // 2, 0, BAND * t * N_KV)
        if dil == t:
            contiguous = (len(views) // 2, win)
        views += [cache_k.reshape(shape), cache_v.reshape(shape)]
        specs += [spec, spec]
    for g, (win, dil) in enumerate(DIL_GROUPS):
        if dil == 1:
            pair, covered = contiguous
            assert covered >= BAND
            sources[g] = (pair, (covered - BAND) * N_KV, BAND * N_KV)
    tok = lambda width: pl.BlockSpec((nb * t, width), lambda i: (i, 0))
    new = pl.BlockSpec((nb * t * N_KV, HEAD_DIM), lambda i: (i, 0))
    bias = _step_bias(t)
    return pl.pallas_call(
        functools.partial(_attn_step_kernel, T=t, nb=nb, sources=tuple(sources[g] for g in range(N_GROUPS))),
        grid=(b // nb,),
        in_specs=[tok(q.shape[1]), new, new] + specs + [pl.BlockSpec(bias.shape, lambda i: (0, 0))],
        out_specs=tok(N_QH_GROUP * HEAD_DIM),
        out_shape=jax.ShapeDtypeStruct((m, N_QH_GROUP * HEAD_DIM), BF16),
        compiler_params=_params("parallel"),
        name="attn_step",
    )(q, k_new.reshape(m * N_KV, HEAD_DIM), v_new.reshape(m * N_KV, HEAD_DIM), *views, bias)


FFN_SPLITS = 2


def _post_mixer_kernel(x_ref, o_ref, p_ref, wo_ref, fg_ref, fin_ref, fout_ref, eg_ref, eproj_ref, egate_ref, y_ref):
    x = x_ref[...] + _dot(o_ref[...], wo_ref[...])
    h = _rms(x, fg_ref[...]).astype(BF16)
    width = FFN_HIDDEN // FFN_SPLITS
    ffn = None
    for c in range(FFN_SPLITS):
        a = _dot(h, fin_ref[:, c * width:(c + 1) * width])
        b = _dot(h, fin_ref[:, FFN_HIDDEN + c * width:FFN_HIDDEN + (c + 1) * width])
        part = _dot((a * jax.nn.sigmoid(a) * b).astype(BF16), fout_ref[c * width:(c + 1) * width, :])
        ffn = part if ffn is None else ffn + part
    x = x + ffn
    h = _rms(x, eg_ref[...]).astype(BF16)
    gate = jax.nn.sigmoid(_dot(h, egate_ref[...]))
    y_ref[...] = x + _dot(p_ref[...].astype(BF16), eproj_ref[...]) * gate


def _post_mixer(x, o, p, wo, fg, fin, fout, eg, eproj, egate, tm):
    m = x.shape[0]
    row = lambda width: pl.BlockSpec((tm, width), lambda i: (i, 0))
    p_stack, layer = p
    return pl.pallas_call(
        _post_mixer_kernel,
        grid=(m // tm,),
        in_specs=[row(D_MODEL), row(o.shape[1]), pl.BlockSpec((None, tm, PLE_DIM), lambda i: (layer, i, 0)),
                  _wspec(wo), _wspec(fg), _wspec(fin), _wspec(fout), _wspec(eg), _wspec(eproj), _wspec(egate)],
        out_specs=row(D_MODEL),
        out_shape=jax.ShapeDtypeStruct((m, D_MODEL), F32),
        compiler_params=_params("parallel"),
        name="post_mixer",
    )(x, o, p_stack, wo[0], fg[0], fin[0], fout[0], eg[0], eproj[0], egate[0])


PROJ_ROWS = 512
POST_ROWS = 512


def _trunk(x, p, gla_s0, cache, pos, w):
    b, l, _ = x.shape
    m = b * l
    depth = p.shape[0]
    n_a = depth // 2
    step = gla_s0 is not None
    xf = x.reshape(m, D_MODEL)
    pf = p.reshape(depth, m, PLE_DIM)
    tm = min(PROJ_ROWS, m)
    tm_post = min(POST_ROWS, m)
    rope = _rope_tables(pos)
    if step:
        rope = jnp.tile(rope, (1, tm // l, 1))
    lw = lambda name, layer: (w[name], layer)
    states = []
    k_new = v_new = None
    for i in range(depth):
        if i < n_a:
            qk, v, r, glr = _proj_a(xf, lw("a_norm", i), lw("a_w_in", i), tm)
            gate_w = (lw("a_w_gk_up", i), lw("a_b_gk", i), lw("a_onorm", i))
            if step:
                o, s = _gla_step(qk, v, r, glr, *gate_w, (gla_s0, i), l, 16 // l)
            else:
                o, s = _gla_seq(qk.reshape(b, l, -1), v.reshape(b, l, -1), r.reshape(b, l, -1), glr.reshape(b, l, -1),
                                *gate_w, T=256)
                o = o.reshape(m, GLA_V)
            states.append(s)
            wo = lw("a_w_o", i)
        else:
            j = i - n_a
            kv_w = N_KV * HEAD_DIM
            if j == 0:
                by_head = [(0, N_KV, True), (N_KV, N_KV, True)]
                flat = [] if step else [(0, N_KV, False), (N_KV, N_KV, False)]
                kv = _head_proj(xf, lw("kv_norm", 0), lw("w_kv", 0), lw("k_norm", 0), rope, N_KV,
                                by_head + flat, (F32,) * (2 + len(flat)), tm)
                k_new, v_new = kv[:2]
            n_q = N_GROUPS * N_QH_GROUP
            (q,) = _head_proj(xf, lw("b_norm", j), lw("b_w_q", j), lw("b_q_norm", j), rope, n_q,
                              [(0, n_q, False)], (F32,), tm)
            if step:
                o = _attn_step(q, k_new, v_new, *cache, l, 8 // l)
            else:
                o = _attn_seq(q.reshape(b, l, -1), kv[2].reshape(b, l, kv_w), kv[3].reshape(b, l, kv_w))
                o = o.reshape(m, N_QH_GROUP * HEAD_DIM)
            wo = lw("b_w_o", j)
        xf = _post_mixer(xf, o, (pf, i), wo, lw("f_norm", i), lw("f_w_in", i), lw("f_w_out", i),
                         lw("e_norm", i), lw("e_w_proj", i), lw("e_w_gate", i), tm_post)
    y = xf.reshape(b, l, D_MODEL)
    return (y, jnp.stack(states), k_new.reshape(b, l, N_KV, HEAD_DIM), v_new.reshape(b, l, N_KV, HEAD_DIM))


def kernel(x_prompt, x_sample, p_prompt, p_sample, state_gla, cache_k, cache_v, a_norm, a_w_in, a_w_gk_up, a_b_gk,
           a_onorm, a_w_o, kv_norm, w_kv, k_norm, b_norm, b_w_q, b_q_norm, b_w_o, f_norm, f_w_in, f_w_out, e_norm,
           e_w_proj, e_w_gate):
    n_a = a_norm.shape[0]
    depth = f_norm.shape[0]
    main = 2 * GLA_QK + 2 * GLA_V
    a_w_in_p = jnp.concatenate(
        [a_w_in[:, :, :main], jnp.pad(a_w_in[:, :, main:], ((0, 0), (0, 0), (0, GLR_PAD - GLA_LR)))], axis=2)
    w = dict(
        a_norm=a_norm.reshape(n_a, 1, D_MODEL),
        a_w_in=a_w_in_p.astype(BF16),
        a_w_gk_up=jnp.pad(a_w_gk_up, ((0, 0), (0, GLR_PAD - GLA_LR), (0, 0))).astype(BF16),
        a_b_gk=a_b_gk.reshape(n_a, 1, GLA_QK),
        a_onorm=a_onorm.reshape(n_a, 1, GLA_DV),
        a_w_o=a_w_o.astype(BF16),
        kv_norm=kv_norm.reshape(1, 1, D_MODEL),
        w_kv=w_kv.astype(BF16)[None],
        k_norm=k_norm.reshape(1, 1, HEAD_DIM),
        b_norm=b_norm.reshape(-1, 1, D_MODEL),
        b_w_q=b_w_q.astype(BF16),
        b_q_norm=b_q_norm.reshape(-1, 1, HEAD_DIM),
        b_w_o=b_w_o.astype(BF16),
        f_norm=f_norm.reshape(depth, 1, D_MODEL),
        f_w_in=f_w_in.astype(BF16),
        f_w_out=f_w_out.astype(BF16),
        e_norm=e_norm.reshape(depth, 1, D_MODEL),
        e_w_proj=e_w_proj.astype(BF16),
        e_w_gate=e_w_gate.astype(BF16),
    )
    seq = x_prompt.shape[1]
    dec_seq = x_sample.shape[1]
    pos_prompt = jnp.arange(seq, dtype=F32)
    pos_sample = PAST_LEN + jnp.arange(dec_seq, dtype=F32)
    y_p, gs_p, k_p, v_p = _trunk(x_prompt, p_prompt, None, None, pos_prompt, w)
    y_s, gs_s, k_s, v_s = _trunk(x_sample, p_sample, state_gla, (cache_k, cache_v), pos_sample, w)
    keep = min(MAX_WINDOW, seq)
    return (y_p, y_s, gs_p, gs_s, k_p[:, seq - keep:], v_p[:, seq - keep:], k_s, v_s)
```

```python
import functools
import math

import jax
import jax.numpy as jnp
import numpy as np
from jax import lax
from jax.experimental import pallas as pl
from jax.experimental.pallas import tpu as pltpu

D_MODEL = 1024
PAST_LEN = 2048
GLA_HEADS = 4
GLA_DK = 128
GLA_DV = 256
GLA_QK = GLA_HEADS * GLA_DK
GLA_V = GLA_HEADS * GLA_DV
GLA_LR = 16
GLA_TAU = 16.0
GLA_CHUNK = 64
HEAD_DIM = 128
N_KV = 4
Q_PER_KV = 2
N_QH_GROUP = N_KV * Q_PER_KV
DIL_GROUPS = ((128, 1), (512, 4), (2048, 16))
N_GROUPS = len(DIL_GROUPS)
MAX_WINDOW = 2048
ROPE_DIM = HEAD_DIM // 4
ROPE_HALF = ROPE_DIM // 2
ROPE_THETA = 500000.0
FFN_HIDDEN = 2816
PLE_DIM = 256
EPS = 1e-6
NEG_INF = -1e30

LANES = 128
GLR_PAD = LANES
VMEM_LIMIT = 56 * 1024 * 1024

F32 = jnp.float32
BF16 = jnp.bfloat16


def _params(*sem):
    return pltpu.CompilerParams(dimension_semantics=sem, vmem_limit_bytes=VMEM_LIMIT)


def _wspec(lw):
    stack, layer = lw
    index = (layer,) + (0,) * (stack.ndim - 1)
    return pl.BlockSpec((None,) + stack.shape[1:], lambda *_: index, pipeline_mode=pl.Buffered(1))


def _rms(x, g):
    var = jnp.mean(x * x, axis=-1, keepdims=True)
    return x * lax.rsqrt(var + EPS) * g


def _log_sigmoid(x):
    return jnp.minimum(x, 0.0) - jnp.log1p(jnp.exp(-jnp.abs(x)))


def _dot(a, b):
    return jnp.dot(a, b, preferred_element_type=F32)


def _dot_nt(a, b):
    return lax.dot_general(a, b, (((1,), (1,)), ((), ())), preferred_element_type=F32)


def _dot_tn(a, b):
    return lax.dot_general(a, b, (((0,), (0,)), ((), ())), preferred_element_type=F32)


def _proj_a_kernel(x_ref, g_ref, w_ref, qk_ref, v_ref, r_ref, glr_ref):
    h = _rms(x_ref[...], g_ref[...]).astype(BF16)
    qk_ref[...] = _dot(h, w_ref[:, 0:2 * GLA_QK])
    v_ref[...] = _dot(h, w_ref[:, 2 * GLA_QK:2 * GLA_QK + GLA_V]).astype(BF16)
    r_ref[...] = _dot(h, w_ref[:, 2 * GLA_QK + GLA_V:2 * GLA_QK + 2 * GLA_V])
    glr_ref[...] = _dot(h, w_ref[:, 2 * GLA_QK + 2 * GLA_V:]).astype(BF16)


def _proj_a(x, g, w, tm):
    m = x.shape[0]
    row = lambda width: pl.BlockSpec((tm, width), lambda i: (i, 0))
    return pl.pallas_call(
        _proj_a_kernel,
        grid=(m // tm,),
        in_specs=[row(D_MODEL), _wspec(g), _wspec(w)],
        out_specs=[row(2 * GLA_QK), row(GLA_V), row(GLA_V), row(GLR_PAD)],
        out_shape=[jax.ShapeDtypeStruct((m, 2 * GLA_QK), F32),
                   jax.ShapeDtypeStruct((m, GLA_V), BF16),
                   jax.ShapeDtypeStruct((m, GLA_V), F32),
                   jax.ShapeDtypeStruct((m, GLR_PAD), BF16)],
        compiler_params=_params("parallel"),
        name="proj_a",
    )(x, g[0], w[0])


def _split3(x):
    hi = x.astype(BF16)
    r1 = x - hi.astype(F32)
    mid = r1.astype(BF16)
    lo = (r1 - mid.astype(F32)).astype(BF16)
    return hi, mid, lo


def _gla_chunk_terms(glr, wgk, bgk, qk, C):
    T = qk.shape[0]
    log_a = _log_sigmoid(_dot(glr, wgk) + bgk) * (1.0 / GLA_TAU)
    row = lax.broadcasted_iota(jnp.int32, (T, T), 0)
    col = lax.broadcasted_iota(jnp.int32, (T, T), 1)
    same_chunk = (row // C) == (col // C)
    causal = same_chunk & (col <= row)
    tril = jnp.where(causal, 1.0, 0.0).astype(BF16)
    chunk_ones = jnp.where(same_chunk, 1.0, 0.0).astype(BF16)
    parts = jnp.concatenate(_split3(log_a), axis=1)
    fold = lambda y: y[:, :GLA_QK] + y[:, GLA_QK:2 * GLA_QK] + y[:, 2 * GLA_QK:]
    cum = fold(_dot(tril, parts))
    total = fold(_dot(chunk_ones, parts))
    q = qk[:, :GLA_QK] * (GLA_DK ** -0.5)
    k = qk[:, GLA_QK:]
    return q * jnp.exp(cum), (k * jnp.exp(-cum)).astype(BF16), k * jnp.exp(total - cum), total, causal


def _state_decay(total_row):
    decay = jnp.exp(jnp.broadcast_to(total_row, (GLA_DK, GLA_DK)).T)
    return jnp.concatenate([decay] * (GLA_DV // GLA_DK), axis=1)


def _gla_gated(o, onorm, gate):
    return _rms(o, onorm) * (gate * jax.nn.sigmoid(gate))


GLA_SEQ_ROWS = 128
GLA_SEQ_BATCH = 4


def _gla_seq_kernel(qk_ref, v_ref, r_ref, glr_ref, wgk_ref, bgk_ref, on_ref, o_ref, s_ref, *, T, C, nb):
    @pl.when(pl.program_id(1) == 0)
    def _():
        s_ref[...] = jnp.zeros_like(s_ref)

    onorm = on_ref[...]
    for e in range(nb):
        q_in, k_in, k_end, total, causal = _gla_chunk_terms(glr_ref[e], wgk_ref[...], bgk_ref[...], qk_ref[e], C)
        q_in, k_end = q_in.astype(BF16), k_end.astype(BF16)
        for h in range(GLA_HEADS):
            lanes = slice(h * GLA_DK, (h + 1) * GLA_DK)
            v = v_ref[e, :, h * GLA_DV:(h + 1) * GLA_DV]
            att = jnp.where(causal, _dot_nt(q_in[:, lanes], k_in[:, lanes]), 0.0).astype(BF16)
            o_intra = _dot(att, v)
            state = s_ref[e, h]
            o_inter = []
            for c in range(T // C):
                rows = slice(c * C, (c + 1) * C)
                o_inter.append(_dot(q_in[rows, lanes], state.astype(BF16)))
                state = state * _state_decay(total[c * C:c * C + 1, lanes]) + _dot_tn(k_end[rows, lanes], v[rows])
            s_ref[e, h] = state
            o = o_intra + jnp.concatenate(o_inter, axis=0)
            og = _gla_gated(o, onorm, r_ref[e, :, h * GLA_DV:(h + 1) * GLA_DV])
            o_ref[e, :, h * GLA_DV:(h + 1) * GLA_DV] = og.astype(o_ref.dtype)


def _gla_seq(qk, v, r, glr, wgk, bgk, onorm):
    b, l, _ = qk.shape
    T, nb = min(GLA_SEQ_ROWS, l), min(GLA_SEQ_BATCH, b)
    tok = lambda width: pl.BlockSpec((nb, T, width), lambda i, t: (i, t, 0))
    return pl.pallas_call(
        functools.partial(_gla_seq_kernel, T=T, C=GLA_CHUNK, nb=nb),
        grid=(b // nb, l // T),
        in_specs=[tok(2 * GLA_QK), tok(GLA_V), tok(GLA_V), tok(GLR_PAD),
                  _wspec(wgk), _wspec(bgk), _wspec(onorm)],
        out_specs=[tok(GLA_V),
                   pl.BlockSpec((nb, GLA_HEADS, GLA_DK, GLA_DV), lambda i, t: (i, 0, 0, 0))],
        out_shape=[jax.ShapeDtypeStruct((b, l, GLA_V), BF16),
                   jax.ShapeDtypeStruct((b, GLA_HEADS, GLA_DK, GLA_DV), F32)],
        compiler_params=_params("parallel", "arbitrary"),
        name="gla_seq",
    )(qk, v, r, glr, wgk[0], bgk[0], onorm[0])


def _gla_step_kernel(qk_ref, v_ref, r_ref, glr_ref, wgk_ref, bgk_ref, on_ref, s0_ref, *rest, nb, L, first):
    o_ref, s_ref = rest[-2:]
    q_in, k_in, k_end, total, causal = _gla_chunk_terms(glr_ref[...], wgk_ref[...], bgk_ref[...], qk_ref[...], L)
    chunk = lax.broadcasted_iota(jnp.int32, (nb * L, 1), 0) // L
    onorm = on_ref[...]
    new_state = s_ref.at[0] if first else s_ref
    if first:
        s_ref[1:] = jnp.zeros((s_ref.shape[0] - 1,) + s_ref.shape[1:], s_ref.dtype)
    for h in range(GLA_HEADS):
        lanes = slice(h * GLA_DK, (h + 1) * GLA_DK)
        v = v_ref[:, h * GLA_DV:(h + 1) * GLA_DV]
        att = jnp.where(causal, _dot_nt(q_in[:, lanes].astype(BF16), k_in[:, lanes]), 0.0).astype(BF16)
        o = _dot(att, v)
        for j in range(nb):
            mine = chunk == j
            state = s0_ref[j, h]
            o = o + _dot(jnp.where(mine, q_in[:, lanes], 0.0).astype(BF16), state.astype(BF16))
            new_state[j, h] = (state * _state_decay(total[j * L:j * L + 1, lanes])
                               + _dot_tn(jnp.where(mine, k_end[:, lanes], 0.0).astype(BF16), v))
        og = _gla_gated(o, onorm, r_ref[:, h * GLA_DV:(h + 1) * GLA_DV])
        o_ref[:, h * GLA_DV:(h + 1) * GLA_DV] = og.astype(o_ref.dtype)


def _gla_step(qk, v, r, glr, wgk, bgk, onorm, s0, s_new, L, nb):
    m = qk.shape[0]
    b = m // L
    rows = nb * L
    tok = lambda width: pl.BlockSpec((rows, width), lambda i: (i, 0))
    s0_stack, layer = s0
    slab = (nb, GLA_HEADS, GLA_DK, GLA_DV)
    st_in = pl.BlockSpec((None,) + slab, lambda i: (layer, i, 0, 0, 0))
    first = s_new is None
    assert first == (layer == 0)
    if first:
        extra_specs, extra_args, aliases = [], [], {}
        st_out = pl.BlockSpec((s0_stack.shape[0],) + slab, lambda i: (0, i, 0, 0, 0))
    else:
        extra_specs, extra_args, aliases = [pl.BlockSpec(memory_space=pl.ANY)], [s_new], {8: 1}
        st_out = pl.BlockSpec((None,) + slab, lambda i: (layer, i, 0, 0, 0))
    return pl.pallas_call(
        functools.partial(_gla_step_kernel, nb=nb, L=L, first=first),
        grid=(b // nb,),
        in_specs=[tok(2 * GLA_QK), tok(GLA_V), tok(GLA_V), tok(GLR_PAD),
                  _wspec(wgk), _wspec(bgk), _wspec(onorm), st_in] + extra_specs,
        out_specs=[tok(GLA_V), st_out],
        out_shape=[jax.ShapeDtypeStruct((m, GLA_V), BF16),
                   jax.ShapeDtypeStruct(s0_stack.shape, F32)],
        input_output_aliases=aliases,
        compiler_params=_params("parallel"),
        name="gla_step",
    )(qk, v, r, glr, wgk[0], bgk[0], onorm[0], s0_stack, *extra_args)


HEADS_PER_DOT = 8


def _head_proj_kernel(x_ref, g_ref, w_ref, hg_ref, rope_ref, *out_refs, n_rot, outs):
    h = _rms(x_ref[...], g_ref[...]).astype(BF16)
    hg = hg_ref[...]
    cos, sin_lo, sin_hi = rope_ref[0], rope_ref[1], rope_ref[2]
    ones = jnp.ones((HEAD_DIM, HEAD_DIM), BF16)
    n_heads = w_ref.shape[1] // HEAD_DIM
    for start in range(0, n_heads, HEADS_PER_DOT):
        stop = min(start + HEADS_PER_DOT, n_heads)
        y = _dot(h, w_ref[:, start * HEAD_DIM:stop * HEAD_DIM])
        for head in range(start, stop):
            yh = y[:, (head - start) * HEAD_DIM:(head - start + 1) * HEAD_DIM]
            if head < n_rot:
                ss = _dot((yh * yh).astype(BF16), ones)
                yn = yh * lax.rsqrt(ss * (1.0 / HEAD_DIM) + EPS) * hg
                yh = (yn * cos + pltpu.roll(yn, HEAD_DIM - ROPE_HALF, axis=1) * sin_lo
                      + pltpu.roll(yn, ROPE_HALF, axis=1) * sin_hi)
            for ref, (first, count, by_head) in zip(out_refs, outs):
                i = head - first
                if 0 <= i < count:
                    if by_head:
                        ref[:, i, :] = yh.astype(ref.dtype)
                    else:
                        ref[:, i * HEAD_DIM:(i + 1) * HEAD_DIM] = yh.astype(ref.dtype)


def _head_proj(x, g, w, hg, rope, n_rot, outs, dtypes, tm):
    m = x.shape[0]
    n_rope_tiles = rope.shape[1] // tm
    row = lambda width: pl.BlockSpec((tm, width), lambda i: (i, 0))
    out_specs, out_shape = [], []
    for (first, count, by_head), dt in zip(outs, dtypes):
        if by_head:
            out_specs.append(pl.BlockSpec((tm, count, HEAD_DIM), lambda i: (i, 0, 0)))
            out_shape.append(jax.ShapeDtypeStruct((m, count, HEAD_DIM), dt))
        else:
            out_specs.append(row(count * HEAD_DIM))
            out_shape.append(jax.ShapeDtypeStruct((m, count * HEAD_DIM), dt))
    return pl.pallas_call(
        functools.partial(_head_proj_kernel, n_rot=n_rot, outs=outs),
        grid=(m // tm,),
        in_specs=[row(D_MODEL), _wspec(g), _wspec(w), _wspec(hg),
                  pl.BlockSpec((3, tm, HEAD_DIM), lambda i: (0, i % n_rope_tiles, 0))],
        out_specs=out_specs,
        out_shape=out_shape,
        compiler_params=_params("parallel"),
        name="head_proj",
    )(x, g[0], w[0], hg[0], rope)


def _rope_tables(pos):
    freqs = jnp.power(ROPE_THETA, -jnp.arange(ROPE_HALF, dtype=F32) * 2.0 / ROPE_DIM)
    ang = pos[:, None] * freqs[None, :]
    cos, sin = jnp.cos(ang), jnp.sin(ang)
    n = pos.shape[0]
    rest = HEAD_DIM - ROPE_DIM
    c = jnp.concatenate([cos, cos, jnp.ones((n, rest), F32)], axis=1)
    s_lo = jnp.concatenate([-sin, jnp.zeros((n, ROPE_HALF + rest), F32)], axis=1)
    s_hi = jnp.concatenate([jnp.zeros((n, ROPE_HALF), F32), sin, jnp.zeros((n, rest), F32)], axis=1)
    return jnp.stack([c, s_lo, s_hi])


BAND = 128


def _attn_seq_kernel(*refs, L):
    n_q = N_GROUPS * Q_PER_KV
    q_refs = refs[:n_q]
    k_ref, v_ref, o_ref, og_ref, lse_ref = refs[n_q:]
    scale = HEAD_DIM ** -0.5
    n2 = Q_PER_KV * BAND
    qi = lax.broadcasted_iota(jnp.int32, (n2, 2 * BAND), 0) % BAND
    kj = lax.broadcasted_iota(jnp.int32, (n2, 2 * BAND), 1)
    band_mask = (kj >= qi) & (kj <= qi + BAND)
    first_mask = (lax.broadcasted_iota(jnp.int32, (n2, BAND), 1)
                  <= lax.broadcasted_iota(jnp.int32, (n2, BAND), 0) % BAND)

    order = sorted(range(N_GROUPS), key=lambda g: -DIL_GROUPS[g][1])
    assert DIL_GROUPS[order[-1]][1] == 1
    slot = {g: i for i, g in enumerate(order[:-1])}
    for g in order:
        win, dil = DIL_GROUPS[g]
        assert win // dil == BAND
        for r in range(dil):
            for blk in range(L // dil // BAND):
                q_rows = pl.ds(r + dil * BAND * blk, BAND, stride=dil)
                q = jnp.concatenate([q_refs[g * Q_PER_KV + j][0, q_rows, :]
                                     for j in range(Q_PER_KV)], axis=0).astype(BF16)
                if blk == 0:
                    k_rows, mask = q_rows, first_mask
                else:
                    k_rows = pl.ds(r + dil * BAND * (blk - 1), 2 * BAND, stride=dil)
                    mask = band_mask
                kb = k_ref[0, k_rows, :].astype(BF16)
                vb = v_ref[0, k_rows, :].astype(BF16)
                s = jnp.where(mask, _dot_nt(q, kb) * scale, NEG_INF)
                for j in range(Q_PER_KV):
                    s_j = s[j * BAND:(j + 1) * BAND]
                    m = jnp.max(s_j, axis=-1, keepdims=True)
                    p = jnp.exp(s_j - m)
                    l = jnp.sum(p, axis=-1, keepdims=True)
                    out = _dot(p.astype(BF16), vb) / l
                    lse = jnp.broadcast_to(m + jnp.log(l), (BAND, LANES))
                    if dil > 1:
                        og_ref[slot[g], j, q_rows, :] = out
                        lse_ref[slot[g], j, q_rows, :] = lse
                    else:
                        rows = slice(blk * BAND, (blk + 1) * BAND)
                        others = [(og_ref[i, j, rows, :], lse_ref[i, j, rows, :]) for i in slot.values()]
                        top = functools.reduce(jnp.maximum, [ls for _, ls in others], lse)
                        w = jnp.exp(lse - top)
                        num, den = w * out, w
                        for o_e, ls_e in others:
                            w = jnp.exp(ls_e - top)
                            num, den = num + w * o_e, den + w
                        o_ref[0, rows, j * HEAD_DIM:(j + 1) * HEAD_DIM] = (num / den).astype(o_ref.dtype)


def _attn_seq(q, k, v):
    b, l, _ = q.shape
    pair = Q_PER_KV * HEAD_DIM
    q_spec = lambda g, j: pl.BlockSpec((1, l, HEAD_DIM), lambda i, h: (i, 0, (g * N_KV + h) * Q_PER_KV + j))
    q_specs = [q_spec(g, j) for g in range(N_GROUPS) for j in range(Q_PER_KV)]
    kv_spec = pl.BlockSpec((1, l, HEAD_DIM), lambda i, h: (i, 0, h))
    stat = pltpu.VMEM((N_GROUPS - 1, Q_PER_KV, l, LANES), F32)
    return pl.pallas_call(
        functools.partial(_attn_seq_kernel, L=l),
        grid=(b, N_KV),
        in_specs=q_specs + [kv_spec, kv_spec],
        out_specs=pl.BlockSpec((1, l, pair), lambda i, h: (i, 0, h)),
        out_shape=jax.ShapeDtypeStruct((b, l, N_QH_GROUP * HEAD_DIM), BF16),
        scratch_shapes=[stat, stat],
        compiler_params=_params("parallel", "arbitrary"),
        name="attn_seq",
    )(*([q] * len(q_specs)), k, v)


def _step_key_rows(dil, t):
    return BAND * (N_KV if dil == 1 else t * N_KV) + BAND


def _step_bias(t):
    slot, ti = np.divmod(np.arange(N_QH_GROUP * t), t)
    kvh = (slot // Q_PER_KV)[:, None]
    ti = ti[:, None]
    new = np.arange(BAND)[None, :]
    new_t, new_h, is_new = new // N_KV, new % N_KV, new < t * N_KV
    parts = []
    for win, dil in DIL_GROUPS:
        rows = np.arange(_step_key_rows(dil, t) - BAND)[None, :]
        if dil == 1:
            ok = (rows % N_KV == kvh) & (rows // N_KV >= ti)
            ok_new = is_new & (new_h == kvh) & (new_t <= ti)
        else:
            ok = rows % (t * N_KV) == ti * N_KV + kvh
            ok_new = is_new & (new_h == kvh) & (new_t == ti)
        parts += [ok, ok_new]
    return jnp.asarray(np.where(np.concatenate(parts, axis=1), 0.0, NEG_INF), F32)


def _attn_step_kernel(*refs, T, nb, sources):
    q_ref, kn_ref, vn_ref = refs[:3]
    bias_ref, o_ref = refs[-2:]
    cached = refs[3:-2]
    scale = HEAD_DIM ** -0.5
    pad = jnp.zeros((BAND - T * N_KV, HEAD_DIM), BF16)
    for bj in range(nb):
        tok = slice(bj * T, (bj + 1) * T)
        new = slice(bj * T * N_KV, (bj + 1) * T * N_KV)
        k_new = jnp.concatenate([kn_ref[new, :].astype(BF16), pad], axis=0)
        v_new = jnp.concatenate([vn_ref[new, :].astype(BF16), pad], axis=0)
        blocks = [(ref[bj].reshape(-1, HEAD_DIM).astype(BF16)) for ref in cached]
        scores, values = [], []
        start = 0
        for g, (pair, first, count) in enumerate(sources):
            keys = jnp.concatenate([blocks[2 * pair][first:first + count], k_new], axis=0)
            values.append(jnp.concatenate([blocks[2 * pair + 1][first:first + count], v_new], axis=0))
            q = jnp.concatenate([q_ref[tok, ((g * N_KV + h) * Q_PER_KV + j) * HEAD_DIM:
                                       ((g * N_KV + h) * Q_PER_KV + j + 1) * HEAD_DIM]
                                 for h in range(N_KV) for j in range(Q_PER_KV)], axis=0)
            n = count + BAND
            scores.append(_dot_nt(q.astype(BF16), keys) * scale + bias_ref[:, start:start + n])
            start += n
        m = functools.reduce(jnp.maximum, [jnp.max(s, axis=-1, keepdims=True) for s in scores])
        probs = [jnp.exp(s - m) for s in scores]
        l = functools.reduce(jnp.add, [jnp.sum(p, axis=-1, keepdims=True) for p in probs])
        acc = functools.reduce(jnp.add, [_dot(p.astype(BF16), v) for p, v in zip(probs, values)])
        out = acc / l
        for slot in range(N_QH_GROUP):
            o_ref[tok, slot * HEAD_DIM:(slot + 1) * HEAD_DIM] = out[slot * T:(slot + 1) * T].astype(o_ref.dtype)


def _attn_step(q, k_new, v_new, cache_k, cache_v, t, nb):
    m = q.shape[0]
    b = m // t
    past = cache_k.shape[1]
    assert all(win // dil == BAND and past >= win for win, dil in DIL_GROUPS)
    assert t <= min(d for _, d in DIL_GROUPS if d > 1) and t <= BAND
    views, specs, sources = [], [], {}
    for g, (win, dil) in enumerate(DIL_GROUPS):
        if dil == 1:
            continue
        shape = (b, past // dil, dil * N_KV, HEAD_DIM)
        assert (past - win) % (dil * BAND) == 0
        blk = (past - win) // (dil * BAND)
        spec = pl.BlockSpec((nb, BAND, t * N_KV, HEAD_DIM), lambda i, blk=blk: (i, blk, 0, 0))
        sources[g] = (len(views) // 2, 0, BAND * t * N_KV)
        if dil == t:
            contiguous = (len(views) // 2, win)
        views += [cache_k.reshape(shape), cache_v.reshape(shape)]
        specs += [spec, spec]
    for g, (win, dil) in enumerate(DIL_GROUPS):
        if dil == 1:
            pair, covered = contiguous
            assert covered >= BAND
            sources[g] = (pair, (covered - BAND) * N_KV, BAND * N_KV)
    tok = lambda width: pl.BlockSpec((nb * t, width), lambda i: (i, 0))
    new = pl.BlockSpec((nb * t * N_KV, HEAD_DIM), lambda i: (i, 0))
    bias = _step_bias(t)
    return pl.pallas_call(
        functools.partial(_attn_step_kernel, T=t, nb=nb, sources=tuple(sources[g] for g in range(N_GROUPS))),
        grid=(b // nb,),
        in_specs=[tok(q.shape[1]), new, new] + specs + [pl.BlockSpec(bias.shape, lambda i: (0, 0))],
        out_specs=tok(N_QH_GROUP * HEAD_DIM),
        out_shape=jax.ShapeDtypeStruct((m, N_QH_GROUP * HEAD_DIM), BF16),
        compiler_params=_params("parallel"),
        name="attn_step",
    )(q, k_new.reshape(m * N_KV, HEAD_DIM), v_new.reshape(m * N_KV, HEAD_DIM), *views, bias)


def _post_mixer_kernel(x_ref, o_ref, p_ref, wo_ref, fg_ref, fin_ref, fout_ref, eg_ref, eproj_ref, egate_ref, y_ref):
    x = x_ref[...] + _dot(o_ref[...], wo_ref[...])
    h = _rms(x, fg_ref[...]).astype(BF16)
    a = _dot(h, fin_ref[:, :FFN_HIDDEN])
    b = _dot(h, fin_ref[:, FFN_HIDDEN:])
    x = x + _dot((a * jax.nn.sigmoid(a) * b).astype(BF16), fout_ref[...])
    h = _rms(x, eg_ref[...]).astype(BF16)
    gate = jax.nn.sigmoid(_dot(h, egate_ref[...]))
    y_ref[...] = x + _dot(p_ref[...].astype(BF16), eproj_ref[...]) * gate


def _post_mixer(x, o, p, wo, fg, fin, fout, eg, eproj, egate, tm):
    m = x.shape[0]
    row = lambda width: pl.BlockSpec((tm, width), lambda i: (i, 0))
    p_stack, layer = p
    return pl.pallas_call(
        _post_mixer_kernel,
        grid=(m // tm,),
        in_specs=[row(D_MODEL), row(o.shape[1]), pl.BlockSpec((None, tm, PLE_DIM), lambda i: (layer, i, 0)),
                  _wspec(wo), _wspec(fg), _wspec(fin), _wspec(fout), _wspec(eg), _wspec(eproj), _wspec(egate)],
        out_specs=row(D_MODEL),
        out_shape=jax.ShapeDtypeStruct((m, D_MODEL), F32),
        compiler_params=_params("parallel"),
        name="post_mixer",
    )(x, o, p_stack, wo[0], fg[0], fin[0], fout[0], eg[0], eproj[0], egate[0])


PROJ_ROWS = 512
POST_ROWS = 512


def _trunk(x, p, gla_s0, cache, pos, w):
    b, l, _ = x.shape
    m = b * l
    depth = p.shape[0]
    n_a = depth // 2
    step = gla_s0 is not None
    xf = x.reshape(m, D_MODEL)
    pf = p.reshape(depth, m, PLE_DIM)
    tm = min(PROJ_ROWS, m)
    tm_post = min(POST_ROWS, m)
    rope = _rope_tables(pos)
    if step:
        rope = jnp.tile(rope, (1, tm // l, 1))
    lw = lambda name, layer: (w[name], layer)
    states, s_stack = [], None
    k_new = v_new = None
    for i in range(depth):
        if i < n_a:
            qk, v, r, glr = _proj_a(xf, lw("a_norm", i), lw("a_w_in", i), tm)
            gate_w = (lw("a_w_gk_up", i), lw("a_b_gk", i), lw("a_onorm", i))
            if step:
                o, s_stack = _gla_step(qk, v, r, glr, *gate_w, (gla_s0, i), s_stack, l, 16 // l)
            else:
                o, s = _gla_seq(qk.reshape(b, l, -1), v.reshape(b, l, -1), r.reshape(b, l, -1), glr.reshape(b, l, -1),
                                *gate_w)
                o = o.reshape(m, GLA_V)
                states.append(s)
            wo = lw("a_w_o", i)
        else:
            j = i - n_a
            kv_w = N_KV * HEAD_DIM
            if j == 0:
                by_head = [(0, N_KV, True), (N_KV, N_KV, True)]
                flat = [] if step else [(0, N_KV, False), (N_KV, N_KV, False)]
                kv = _head_proj(xf, lw("kv_norm", 0), lw("w_kv", 0), lw("k_norm", 0), rope, N_KV,
                                by_head + flat, (F32,) * (2 + len(flat)), tm)
                k_new, v_new = kv[:2]
            n_q = N_GROUPS * N_QH_GROUP
            (q,) = _head_proj(xf, lw("b_norm", j), lw("b_w_q", j), lw("b_q_norm", j), rope, n_q,
                              [(0, n_q, False)], (F32,), tm)
            if step:
                o = _attn_step(q, k_new, v_new, *cache, l, 8 // l)
            else:
                o = _attn_seq(q.reshape(b, l, -1), kv[2].reshape(b, l, kv_w), kv[3].reshape(b, l, kv_w))
                o = o.reshape(m, N_QH_GROUP * HEAD_DIM)
            wo = lw("b_w_o", j)
        xf = _post_mixer(xf, o, (pf, i), wo, lw("f_norm", i), lw("f_w_in", i), lw("f_w_out", i),
                         lw("e_norm", i), lw("e_w_proj", i), lw("e_w_gate", i), tm_post)
    y = xf.reshape(b, l, D_MODEL)
    return (y, s_stack if step else jnp.stack(states), k_new.reshape(b, l, N_KV, HEAD_DIM), v_new.reshape(b, l, N_KV, HEAD_DIM))


def kernel(x_prompt, x_sample, p_prompt, p_sample, state_gla, cache_k, cache_v, a_norm, a_w_in, a_w_gk_up, a_b_gk,
           a_onorm, a_w_o, kv_norm, w_kv, k_norm, b_norm, b_w_q, b_q_norm, b_w_o, f_norm, f_w_in, f_w_out, e_norm,
           e_w_proj, e_w_gate):
    n_a = a_norm.shape[0]
    depth = f_norm.shape[0]
    main = 2 * GLA_QK + 2 * GLA_V
    a_w_in_p = jnp.concatenate(
        [a_w_in[:, :, :main], jnp.pad(a_w_in[:, :, main:], ((0, 0), (0, 0), (0, GLR_PAD - GLA_LR)))], axis=2)
    w = dict(
        a_norm=a_norm.reshape(n_a, 1, D_MODEL),
        a_w_in=a_w_in_p.astype(BF16),
        a_w_gk_up=jnp.pad(a_w_gk_up, ((0, 0), (0, GLR_PAD - GLA_LR), (0, 0))).astype(BF16),
        a_b_gk=a_b_gk.reshape(n_a, 1, GLA_QK),
        a_onorm=a_onorm.reshape(n_a, 1, GLA_DV),
        a_w_o=a_w_o.astype(BF16),
        kv_norm=kv_norm.reshape(1, 1, D_MODEL),
        w_kv=w_kv.astype(BF16)[None],
        k_norm=k_norm.reshape(1, 1, HEAD_DIM),
        b_norm=b_norm.reshape(-1, 1, D_MODEL),
        b_w_q=b_w_q.astype(BF16),
        b_q_norm=b_q_norm.reshape(-1, 1, HEAD_DIM),
        b_w_o=b_w_o.astype(BF16),
        f_norm=f_norm.reshape(depth, 1, D_MODEL),
        f_w_in=f_w_in.astype(BF16),
        f_w_out=f_w_out.astype(BF16),
        e_norm=e_norm.reshape(depth, 1, D_MODEL),
        e_w_proj=e_w_proj.astype(BF16),
        e_w_gate=e_w_gate.astype(BF16),
    )
    seq = x_prompt.shape[1]
    dec_seq = x_sample.shape[1]
    pos_prompt = jnp.arange(seq, dtype=F32)
    pos_sample = PAST_LEN + jnp.arange(dec_seq, dtype=F32)
    y_p, gs_p, k_p, v_p = _trunk(x_prompt, p_prompt, None, None, pos_prompt, w)
    y_s, gs_s, k_s, v_s = _trunk(x_sample, p_sample, state_gla, (cache_k, cache_v), pos_sample, w)
    keep = min(MAX_WINDOW, seq)
    return (y_p, y_s, gs_p, gs_s, k_p[:, seq - keep:], v_p[:, seq - keep:], k_s, v_s)
```

```python
import functools
import math

import jax
import jax.numpy as jnp
import numpy as np
from jax import lax
from jax.experimental import pallas as pl
from jax.experimental.pallas import tpu as pltpu

D_MODEL = 1024
PAST_LEN = 2048
GLA_HEADS = 4
GLA_DK = 128
GLA_DV = 256
GLA_QK = GLA_HEADS * GLA_DK
GLA_V = GLA_HEADS * GLA_DV
GLA_LR = 16
GLA_TAU = 16.0
GLA_CHUNK = 64
HEAD_DIM = 128
N_KV = 4
Q_PER_KV = 2
N_QH_GROUP = N_KV * Q_PER_KV
DIL_GROUPS = ((128, 1), (512, 4), (2048, 16))
N_GROUPS = len(DIL_GROUPS)
MAX_WINDOW = 2048
ROPE_DIM = HEAD_DIM // 4
ROPE_HALF = ROPE_DIM // 2
ROPE_THETA = 500000.0
FFN_HIDDEN = 2816
PLE_DIM = 256
EPS = 1e-6
NEG_INF = -1e30

LANES = 128
GLR_PAD = LANES
VMEM_LIMIT = 56 * 1024 * 1024

F32 = jnp.float32
BF16 = jnp.bfloat16


def _params(*sem):
    return pltpu.CompilerParams(dimension_semantics=sem, vmem_limit_bytes=VMEM_LIMIT)


def _wspec(lw):
    stack, layer = lw
    index = (layer,) + (0,) * (stack.ndim - 1)
    return pl.BlockSpec((None,) + stack.shape[1:], lambda *_: index, pipeline_mode=pl.Buffered(1))


def _rms(x, g):
    var = jnp.mean(x * x, axis=-1, keepdims=True)
    return x * lax.rsqrt(var + EPS) * g


def _log_sigmoid(x):
    return jnp.minimum(x, 0.0) - jnp.log1p(jnp.exp(-jnp.abs(x)))


def _dot(a, b):
    return jnp.dot(a, b, preferred_element_type=F32)


def _dot_nt(a, b):
    return lax.dot_general(a, b, (((1,), (1,)), ((), ())), preferred_element_type=F32)


def _dot_tn(a, b):
    return lax.dot_general(a, b, (((0,), (0,)), ((), ())), preferred_element_type=F32)


def _proj_a_kernel(x_ref, g_ref, w_ref, qk_ref, v_ref, r_ref, glr_ref):
    h = _rms(x_ref[...], g_ref[...]).astype(BF16)
    qk_ref[...] = _dot(h, w_ref[:, 0:2 * GLA_QK])
    v_ref[...] = _dot(h, w_ref[:, 2 * GLA_QK:2 * GLA_QK + GLA_V]).astype(BF16)
    r_ref[...] = _dot(h, w_ref[:, 2 * GLA_QK + GLA_V:2 * GLA_QK + 2 * GLA_V])
    glr_ref[...] = _dot(h, w_ref[:, 2 * GLA_QK + 2 * GLA_V:]).astype(BF16)


def _proj_a(x, g, w, tm):
    m = x.shape[0]
    row = lambda width: pl.BlockSpec((tm, width), lambda i: (i, 0))
    return pl.pallas_call(
        _proj_a_kernel,
        grid=(m // tm,),
        in_specs=[row(D_MODEL), _wspec(g), _wspec(w)],
        out_specs=[row(2 * GLA_QK), row(GLA_V), row(GLA_V), row(GLR_PAD)],
        out_shape=[jax.ShapeDtypeStruct((m, 2 * GLA_QK), F32),
                   jax.ShapeDtypeStruct((m, GLA_V), BF16),
                   jax.ShapeDtypeStruct((m, GLA_V), F32),
                   jax.ShapeDtypeStruct((m, GLR_PAD), BF16)],
        compiler_params=_params("parallel"),
        name="proj_a",
    )(x, g[0], w[0])


def _split3(x):
    hi = x.astype(BF16)
    r1 = x - hi.astype(F32)
    mid = r1.astype(BF16)
    lo = (r1 - mid.astype(F32)).astype(BF16)
    return hi, mid, lo


def _gla_chunk_terms(glr, wgk, bgk, qk, C):
    T = qk.shape[0]
    log_a = _log_sigmoid(_dot(glr, wgk) + bgk) * (1.0 / GLA_TAU)
    row = lax.broadcasted_iota(jnp.int32, (T, T), 0)
    col = lax.broadcasted_iota(jnp.int32, (T, T), 1)
    same_chunk = (row // C) == (col // C)
    causal = same_chunk & (col <= row)
    tril = jnp.where(causal, 1.0, 0.0).astype(BF16)
    chunk_ones = jnp.where(same_chunk, 1.0, 0.0).astype(BF16)
    parts = jnp.concatenate(_split3(log_a), axis=1)
    fold = lambda y: y[:, :GLA_QK] + y[:, GLA_QK:2 * GLA_QK] + y[:, 2 * GLA_QK:]
    cum = fold(_dot(tril, parts))
    total = fold(_dot(chunk_ones, parts))
    q = qk[:, :GLA_QK] * (GLA_DK ** -0.5)
    k = qk[:, GLA_QK:]
    return q * jnp.exp(cum), (k * jnp.exp(-cum)).astype(BF16), k * jnp.exp(total - cum), total, causal


def _state_decay(total_row):
    decay = jnp.exp(jnp.broadcast_to(total_row, (GLA_DK, GLA_DK)).T)
    return jnp.concatenate([decay] * (GLA_DV // GLA_DK), axis=1)


def _gla_gated(o, onorm, gate):
    return _rms(o, onorm) * (gate * jax.nn.sigmoid(gate))


GLA_SEQ_ROWS = 128
GLA_SEQ_BATCH = 4


def _gla_seq_kernel(qk_ref, v_ref, r_ref, glr_ref, wgk_ref, bgk_ref, on_ref, o_ref, s_ref, *, T, C, nb):
    @pl.when(pl.program_id(1) == 0)
    def _():
        s_ref[...] = jnp.zeros_like(s_ref)

    onorm = on_ref[...]
    for e in range(nb):
        q_in, k_in, k_end, total, causal = _gla_chunk_terms(glr_ref[e], wgk_ref[...], bgk_ref[...], qk_ref[e], C)
        q_in, k_end = q_in.astype(BF16), k_end.astype(BF16)
        for h in range(GLA_HEADS):
            lanes = slice(h * GLA_DK, (h + 1) * GLA_DK)
            v = v_ref[e, :, h * GLA_DV:(h + 1) * GLA_DV]
            att = jnp.where(causal, _dot_nt(q_in[:, lanes], k_in[:, lanes]), 0.0).astype(BF16)
            o_intra = _dot(att, v)
            state = s_ref[e, h]
            o_inter = []
            for c in range(T // C):
                rows = slice(c * C, (c + 1) * C)
                o_inter.append(_dot(q_in[rows, lanes], state.astype(BF16)))
                state = state * _state_decay(total[c * C:c * C + 1, lanes]) + _dot_tn(k_end[rows, lanes], v[rows])
            s_ref[e, h] = state
            o = o_intra + jnp.concatenate(o_inter, axis=0)
            og = _gla_gated(o, onorm, r_ref[e, :, h * GLA_DV:(h + 1) * GLA_DV])
            o_ref[e, :, h * GLA_DV:(h + 1) * GLA_DV] = og.astype(o_ref.dtype)


def _gla_seq(qk, v, r, glr, wgk, bgk, onorm):
    b, l, _ = qk.shape
    T, nb = min(GLA_SEQ_ROWS, l), min(GLA_SEQ_BATCH, b)
    tok = lambda width: pl.BlockSpec((nb, T, width), lambda i, t: (i, t, 0))
    return pl.pallas_call(
        functools.partial(_gla_seq_kernel, T=T, C=GLA_CHUNK, nb=nb),
        grid=(b // nb, l // T),
        in_specs=[tok(2 * GLA_QK), tok(GLA_V), tok(GLA_V), tok(GLR_PAD),
                  _wspec(wgk), _wspec(bgk), _wspec(onorm)],
        out_specs=[tok(GLA_V),
                   pl.BlockSpec((nb, GLA_HEADS, GLA_DK, GLA_DV), lambda i, t: (i, 0, 0, 0))],
        out_shape=[jax.ShapeDtypeStruct((b, l, GLA_V), BF16),
                   jax.ShapeDtypeStruct((b, GLA_HEADS, GLA_DK, GLA_DV), F32)],
        compiler_params=_params("parallel", "arbitrary"),
        name="gla_seq",
    )(qk, v, r, glr, wgk[0], bgk[0], onorm[0])


def _gla_step_kernel(qk_ref, v_ref, r_ref, glr_ref, wgk_ref, bgk_ref, on_ref, s0_ref, *rest, nb, L, first):
    o_ref, s_ref = rest[-2:]
    q_in, k_in, k_end, total, causal = _gla_chunk_terms(glr_ref[...], wgk_ref[...], bgk_ref[...], qk_ref[...], L)
    chunk = lax.broadcasted_iota(jnp.int32, (nb * L, 1), 0) // L
    onorm = on_ref[...]
    new_state = s_ref.at[0] if first else s_ref
    if first:
        s_ref[1:] = jnp.zeros((s_ref.shape[0] - 1,) + s_ref.shape[1:], s_ref.dtype)
    for h in range(GLA_HEADS):
        lanes = slice(h * GLA_DK, (h + 1) * GLA_DK)
        v = v_ref[:, h * GLA_DV:(h + 1) * GLA_DV]
        att = jnp.where(causal, _dot_nt(q_in[:, lanes].astype(BF16), k_in[:, lanes]), 0.0).astype(BF16)
        o = _dot(att, v)
        for j in range(nb):
            mine = chunk == j
            state = s0_ref[j, h]
            o = o + _dot(jnp.where(mine, q_in[:, lanes], 0.0).astype(BF16), state.astype(BF16))
            new_state[j, h] = (state * _state_decay(total[j * L:j * L + 1, lanes])
                               + _dot_tn(jnp.where(mine, k_end[:, lanes], 0.0).astype(BF16), v))
        og = _gla_gated(o, onorm, r_ref[:, h * GLA_DV:(h + 1) * GLA_DV])
        o_ref[:, h * GLA_DV:(h + 1) * GLA_DV] = og.astype(o_ref.dtype)


def _gla_step(qk, v, r, glr, wgk, bgk, onorm, s0, s_new, L, nb):
    m = qk.shape[0]
    b = m // L
    rows = nb * L
    tok = lambda width: pl.BlockSpec((rows, width), lambda i: (i, 0))
    s0_stack, layer = s0
    slab = (nb, GLA_HEADS, GLA_DK, GLA_DV)
    st_in = pl.BlockSpec((None,) + slab, lambda i: (layer, i, 0, 0, 0))
    first = s_new is None
    assert first == (layer == 0)
    if first:
        extra_specs, extra_args, aliases = [], [], {}
        st_out = pl.BlockSpec((s0_stack.shape[0],) + slab, lambda i: (0, i, 0, 0, 0))
    else:
        extra_specs, extra_args, aliases = [pl.BlockSpec(memory_space=pl.ANY)], [s_new], {8: 1}
        st_out = pl.BlockSpec((None,) + slab, lambda i: (layer, i, 0, 0, 0))
    return pl.pallas_call(
        functools.partial(_gla_step_kernel, nb=nb, L=L, first=first),
        grid=(b // nb,),
        in_specs=[tok(2 * GLA_QK), tok(GLA_V), tok(GLA_V), tok(GLR_PAD),
                  _wspec(wgk), _wspec(bgk), _wspec(onorm), st_in] + extra_specs,
        out_specs=[tok(GLA_V), st_out],
        out_shape=[jax.ShapeDtypeStruct((m, GLA_V), BF16),
                   jax.ShapeDtypeStruct(s0_stack.shape, F32)],
        input_output_aliases=aliases,
        compiler_params=_params("parallel"),
        name="gla_step",
    )(qk, v, r, glr, wgk[0], bgk[0], onorm[0], s0_stack, *extra_args)


HEADS_PER_DOT = 8


def _head_proj_kernel(x_ref, g_ref, w_ref, hg_ref, rope_ref, *out_refs, n_rot, outs):
    h = _rms(x_ref[...], g_ref[...]).astype(BF16)
    hg = hg_ref[...]
    cos, sin_lo, sin_hi = rope_ref[0], rope_ref[1], rope_ref[2]
    ones = jnp.ones((HEAD_DIM, HEAD_DIM), BF16)
    n_heads = w_ref.shape[1] // HEAD_DIM
    for start in range(0, n_heads, HEADS_PER_DOT):
        stop = min(start + HEADS_PER_DOT, n_heads)
        y = _dot(h, w_ref[:, start * HEAD_DIM:stop * HEAD_DIM])
        for head in range(start, stop):
            yh = y[:, (head - start) * HEAD_DIM:(head - start + 1) * HEAD_DIM]
            if head < n_rot:
                ss = _dot((yh * yh).astype(BF16), ones)
                yn = yh * lax.rsqrt(ss * (1.0 / HEAD_DIM) + EPS) * hg
                yh = (yn * cos + pltpu.roll(yn, HEAD_DIM - ROPE_HALF, axis=1) * sin_lo
                      + pltpu.roll(yn, ROPE_HALF, axis=1) * sin_hi)
            for ref, (first, count, by_head) in zip(out_refs, outs):
                i = head - first
                if 0 <= i < count:
                    if by_head:
                        ref[pl.ds(i, yh.shape[0], stride=count), :] = yh.astype(ref.dtype)
                    else:
                        ref[:, i * HEAD_DIM:(i + 1) * HEAD_DIM] = yh.astype(ref.dtype)


def _head_proj(x, g, w, hg, rope, n_rot, outs, dtypes, tm):
    m = x.shape[0]
    n_rope_tiles = rope.shape[1] // tm
    row = lambda width: pl.BlockSpec((tm, width), lambda i: (i, 0))
    out_specs, out_shape = [], []
    for (first, count, by_head), dt in zip(outs, dtypes):
        if by_head:
            out_specs.append(pl.BlockSpec((tm * count, HEAD_DIM), lambda i: (i, 0)))
            out_shape.append(jax.ShapeDtypeStruct((m * count, HEAD_DIM), dt))
        else:
            out_specs.append(row(count * HEAD_DIM))
            out_shape.append(jax.ShapeDtypeStruct((m, count * HEAD_DIM), dt))
    return pl.pallas_call(
        functools.partial(_head_proj_kernel, n_rot=n_rot, outs=outs),
        grid=(m // tm,),
        in_specs=[row(D_MODEL), _wspec(g), _wspec(w), _wspec(hg),
                  pl.BlockSpec((3, tm, HEAD_DIM), lambda i: (0, i % n_rope_tiles, 0))],
        out_specs=out_specs,
        out_shape=out_shape,
        compiler_params=_params("parallel"),
        name="head_proj",
    )(x, g[0], w[0], hg[0], rope)


def _rope_tables(pos):
    freqs = jnp.power(ROPE_THETA, -jnp.arange(ROPE_HALF, dtype=F32) * 2.0 / ROPE_DIM)
    ang = pos[:, None] * freqs[None, :]
    cos, sin = jnp.cos(ang), jnp.sin(ang)
    n = pos.shape[0]
    rest = HEAD_DIM - ROPE_DIM
    c = jnp.concatenate([cos, cos, jnp.ones((n, rest), F32)], axis=1)
    s_lo = jnp.concatenate([-sin, jnp.zeros((n, ROPE_HALF + rest), F32)], axis=1)
    s_hi = jnp.concatenate([jnp.zeros((n, ROPE_HALF), F32), sin, jnp.zeros((n, rest), F32)], axis=1)
    return jnp.stack([c, s_lo, s_hi])


BAND = 128


def _attn_seq_kernel(*refs, L):
    n_q = N_GROUPS * Q_PER_KV
    q_refs = refs[:n_q]
    k_ref, v_ref, o_ref, og_ref, lse_ref = refs[n_q:]
    scale = HEAD_DIM ** -0.5
    n2 = Q_PER_KV * BAND
    qi = lax.broadcasted_iota(jnp.int32, (n2, 2 * BAND), 0) % BAND
    kj = lax.broadcasted_iota(jnp.int32, (n2, 2 * BAND), 1)
    band_mask = (kj >= qi) & (kj <= qi + BAND)
    first_mask = (lax.broadcasted_iota(jnp.int32, (n2, BAND), 1)
                  <= lax.broadcasted_iota(jnp.int32, (n2, BAND), 0) % BAND)

    order = sorted(range(N_GROUPS), key=lambda g: -DIL_GROUPS[g][1])
    assert DIL_GROUPS[order[-1]][1] == 1
    slot = {g: i for i, g in enumerate(order[:-1])}
    for g in order:
        win, dil = DIL_GROUPS[g]
        assert win // dil == BAND
        for r in range(dil):
            for blk in range(L // dil // BAND):
                q_rows = pl.ds(r + dil * BAND * blk, BAND, stride=dil)
                q = jnp.concatenate([q_refs[g * Q_PER_KV + j][0, q_rows, :]
                                     for j in range(Q_PER_KV)], axis=0).astype(BF16)
                if blk == 0:
                    k_rows, mask = q_rows, first_mask
                else:
                    k_rows = pl.ds(r + dil * BAND * (blk - 1), 2 * BAND, stride=dil)
                    mask = band_mask
                kb = k_ref[0, k_rows, :].astype(BF16)
                vb = v_ref[0, k_rows, :].astype(BF16)
                s = jnp.where(mask, _dot_nt(q, kb) * scale, NEG_INF)
                for j in range(Q_PER_KV):
                    s_j = s[j * BAND:(j + 1) * BAND]
                    m = jnp.max(s_j, axis=-1, keepdims=True)
                    p = jnp.exp(s_j - m)
                    l = jnp.sum(p, axis=-1, keepdims=True)
                    out = _dot(p.astype(BF16), vb) / l
                    lse = jnp.broadcast_to(m + jnp.log(l), (BAND, LANES))
                    if dil > 1:
                        og_ref[slot[g], j, q_rows, :] = out
                        lse_ref[slot[g], j, q_rows, :] = lse
                    else:
                        rows = slice(blk * BAND, (blk + 1) * BAND)
                        others = [(og_ref[i, j, rows, :], lse_ref[i, j, rows, :]) for i in slot.values()]
                        top = functools.reduce(jnp.maximum, [ls for _, ls in others], lse)
                        w = jnp.exp(lse - top)
                        num, den = w * out, w
                        for o_e, ls_e in others:
                            w = jnp.exp(ls_e - top)
                            num, den = num + w * o_e, den + w
                        o_ref[0, rows, j * HEAD_DIM:(j + 1) * HEAD_DIM] = (num / den).astype(o_ref.dtype)


def _attn_seq(q, k, v):
    b, l, _ = q.shape
    pair = Q_PER_KV * HEAD_DIM
    q_spec = lambda g, j: pl.BlockSpec((1, l, HEAD_DIM), lambda i, h: (i, 0, (g * N_KV + h) * Q_PER_KV + j))
    q_specs = [q_spec(g, j) for g in range(N_GROUPS) for j in range(Q_PER_KV)]
    kv_spec = pl.BlockSpec((1, l, HEAD_DIM), lambda i, h: (i, 0, h))
    stat = pltpu.VMEM((N_GROUPS - 1, Q_PER_KV, l, LANES), F32)
    return pl.pallas_call(
        functools.partial(_attn_seq_kernel, L=l),
        grid=(b, N_KV),
        in_specs=q_specs + [kv_spec, kv_spec],
        out_specs=pl.BlockSpec((1, l, pair), lambda i, h: (i, 0, h)),
        out_shape=jax.ShapeDtypeStruct((b, l, N_QH_GROUP * HEAD_DIM), BF16),
        scratch_shapes=[stat, stat],
        compiler_params=_params("parallel", "arbitrary"),
        name="attn_seq",
    )(*([q] * len(q_specs)), k, v)


def _step_key_rows(dil, t):
    return BAND * (N_KV if dil == 1 else t * N_KV) + BAND


def _step_bias(t):
    slot, ti = np.divmod(np.arange(N_QH_GROUP * t), t)
    kvh = (slot // Q_PER_KV)[:, None]
    ti = ti[:, None]
    new = np.arange(BAND)[None, :]
    new_t, new_h, is_new = new // N_KV, new % N_KV, new < t * N_KV
    parts = []
    for win, dil in DIL_GROUPS:
        rows = np.arange(_step_key_rows(dil, t) - BAND)[None, :]
        if dil == 1:
            ok = (rows % N_KV == kvh) & (rows // N_KV >= ti)
            ok_new = is_new & (new_h == kvh) & (new_t <= ti)
        else:
            ok = rows % (t * N_KV) == ti * N_KV + kvh
            ok_new = is_new & (new_h == kvh) & (new_t == ti)
        parts += [ok, ok_new]
    return jnp.asarray(np.where(np.concatenate(parts, axis=1), 0.0, NEG_INF), F32)


def _attn_step_kernel(*refs, T, nb, sources):
    q_ref, kn_ref, vn_ref = refs[:3]
    bias_ref, o_ref = refs[-2:]
    cached = refs[3:-2]
    scale = HEAD_DIM ** -0.5
    pad = jnp.zeros((BAND - T * N_KV, HEAD_DIM), BF16)
    for bj in range(nb):
        tok = slice(bj * T, (bj + 1) * T)
        new = slice(bj * T * N_KV, (bj + 1) * T * N_KV)
        k_new = jnp.concatenate([kn_ref[new, :].astype(BF16), pad], axis=0)
        v_new = jnp.concatenate([vn_ref[new, :].astype(BF16), pad], axis=0)
        blocks = [(ref[bj].reshape(-1, HEAD_DIM).astype(BF16)) for ref in cached]
        scores, values = [], []
        start = 0
        for g, (pair, first, count) in enumerate(sources):
            keys = jnp.concatenate([blocks[2 * pair][first:first + count], k_new], axis=0)
            values.append(jnp.concatenate([blocks[2 * pair + 1][first:first + count], v_new], axis=0))
            q = jnp.concatenate([q_ref[tok, ((g * N_KV + h) * Q_PER_KV + j) * HEAD_DIM:
                                       ((g * N_KV + h) * Q_PER_KV + j + 1) * HEAD_DIM]
                                 for h in range(N_KV) for j in range(Q_PER_KV)], axis=0)
            n = count + BAND
            scores.append(_dot_nt(q.astype(BF16), keys) * scale + bias_ref[:, start:start + n])
            start += n
        m = functools.reduce(jnp.maximum, [jnp.max(s, axis=-1, keepdims=True) for s in scores])
        probs = [jnp.exp(s - m) for s in scores]
        l = functools.reduce(jnp.add, [jnp.sum(p, axis=-1, keepdims=True) for p in probs])
        acc = functools.reduce(jnp.add, [_dot(p.astype(BF16), v) for p, v in zip(probs, values)])
        out = acc / l
        for slot in range(N_QH_GROUP):
            o_ref[tok, slot * HEAD_DIM:(slot + 1) * HEAD_DIM] = out[slot * T:(slot + 1) * T].astype(o_ref.dtype)


def _attn_step(q, k_new, v_new, cache_k, cache_v, t, nb):
    m = q.shape[0]
    b = m // t
    past = cache_k.shape[1]
    assert all(win // dil == BAND and past >= win for win, dil in DIL_GROUPS)
    assert t <= min(d for _, d in DIL_GROUPS if d > 1) and t <= BAND
    views, specs, sources = [], [], {}
    for g, (win, dil) in enumerate(DIL_GROUPS):
        if dil == 1:
            continue
        shape = (b, past // dil, dil * N_KV, HEAD_DIM)
        assert (past - win) % (dil * BAND) == 0
        blk = (past - win) // (dil * BAND)
        spec = pl.BlockSpec((nb, BAND, t * N_KV, HEAD_DIM), lambda i, blk=blk: (i, blk, 0, 0))
        sources[g] = (len(views) // 2, 0, BAND * t * N_KV)
        if dil == t:
            contiguous = (len(views) // 2, win)
        views += [cache_k.reshape(shape), cache_v.reshape(shape)]
        specs += [spec, spec]
    for g, (win, dil) in enumerate(DIL_GROUPS):
        if dil == 1:
            pair, covered = contiguous
            assert covered >= BAND
            sources[g] = (pair, (covered - BAND) * N_KV, BAND * N_KV)
    tok = lambda width: pl.BlockSpec((nb * t, width), lambda i: (i, 0))
    new = pl.BlockSpec((nb * t * N_KV, HEAD_DIM), lambda i: (i, 0))
    bias = _step_bias(t)
    return pl.pallas_call(
        functools.partial(_attn_step_kernel, T=t, nb=nb, sources=tuple(sources[g] for g in range(N_GROUPS))),
        grid=(b // nb,),
        in_specs=[tok(q.shape[1]), new, new] + specs + [pl.BlockSpec(bias.shape, lambda i: (0, 0))],
        out_specs=tok(N_QH_GROUP * HEAD_DIM),
        out_shape=jax.ShapeDtypeStruct((m, N_QH_GROUP * HEAD_DIM), BF16),
        compiler_params=_params("parallel"),
        name="attn_step",
    )(q, k_new, v_new, *views, bias)


def _post_mixer_kernel(x_ref, o_ref, p_ref, wo_ref, fg_ref, fin_ref, fout_ref, eg_ref, eproj_ref, egate_ref, y_ref):
    x = x_ref[...] + _dot(o_ref[...], wo_ref[...])
    h = _rms(x, fg_ref[...]).astype(BF16)
    a = _dot(h, fin_ref[:, :FFN_HIDDEN])
    b = _dot(h, fin_ref[:, FFN_HIDDEN:])
    x = x + _dot((a * jax.nn.sigmoid(a) * b).astype(BF16), fout_ref[...])
    h = _rms(x, eg_ref[...]).astype(BF16)
    gate = jax.nn.sigmoid(_dot(h, egate_ref[...]))
    y_ref[...] = x + _dot(p_ref[...].astype(BF16), eproj_ref[...]) * gate


def _post_mixer(x, o, p, wo, fg, fin, fout, eg, eproj, egate, tm):
    m = x.shape[0]
    row = lambda width: pl.BlockSpec((tm, width), lambda i: (i, 0))
    p_stack, layer = p
    return pl.pallas_call(
        _post_mixer_kernel,
        grid=(m // tm,),
        in_specs=[row(D_MODEL), row(o.shape[1]), pl.BlockSpec((None, tm, PLE_DIM), lambda i: (layer, i, 0)),
                  _wspec(wo), _wspec(fg), _wspec(fin), _wspec(fout), _wspec(eg), _wspec(eproj), _wspec(egate)],
        out_specs=row(D_MODEL),
        out_shape=jax.ShapeDtypeStruct((m, D_MODEL), F32),
        compiler_params=_params("parallel"),
        name="post_mixer",
    )(x, o, p_stack, wo[0], fg[0], fin[0], fout[0], eg[0], eproj[0], egate[0])


PROJ_ROWS = 512
POST_ROWS = 512


def _trunk(x, p, gla_s0, cache, pos, w):
    b, l, _ = x.shape
    m = b * l
    depth = p.shape[0]
    n_a = depth // 2
    step = gla_s0 is not None
    xf = x.reshape(m, D_MODEL)
    pf = p.reshape(depth, m, PLE_DIM)
    tm = min(PROJ_ROWS, m)
    tm_post = min(POST_ROWS, m)
    rope = _rope_tables(pos)
    if step:
        rope = jnp.tile(rope, (1, tm // l, 1))
    lw = lambda name, layer: (w[name], layer)
    states, s_stack = [], None
    k_new = v_new = None
    for i in range(depth):
        if i < n_a:
            qk, v, r, glr = _proj_a(xf, lw("a_norm", i), lw("a_w_in", i), tm)
            gate_w = (lw("a_w_gk_up", i), lw("a_b_gk", i), lw("a_onorm", i))
            if step:
                o, s_stack = _gla_step(qk, v, r, glr, *gate_w, (gla_s0, i), s_stack, l, 16 // l)
            else:
                o, s = _gla_seq(qk.reshape(b, l, -1), v.reshape(b, l, -1), r.reshape(b, l, -1), glr.reshape(b, l, -1),
                                *gate_w)
                o = o.reshape(m, GLA_V)
                states.append(s)
            wo = lw("a_w_o", i)
        else:
            j = i - n_a
            kv_w = N_KV * HEAD_DIM
            if j == 0:
                by_head = [(0, N_KV, True), (N_KV, N_KV, True)]
                flat = [] if step else [(0, N_KV, False), (N_KV, N_KV, False)]
                kv = _head_proj(xf, lw("kv_norm", 0), lw("w_kv", 0), lw("k_norm", 0), rope, N_KV,
                                by_head + flat, (F32,) * (2 + len(flat)), tm)
                k_new, v_new = kv[:2]
            n_q = N_GROUPS * N_QH_GROUP
            (q,) = _head_proj(xf, lw("b_norm", j), lw("b_w_q", j), lw("b_q_norm", j), rope, n_q,
                              [(0, n_q, False)], (F32,), tm)
            if step:
                o = _attn_step(q, k_new, v_new, *cache, l, 8 // l)
            else:
                o = _attn_seq(q.reshape(b, l, -1), kv[2].reshape(b, l, kv_w), kv[3].reshape(b, l, kv_w))
                o = o.reshape(m, N_QH_GROUP * HEAD_DIM)
            wo = lw("b_w_o", j)
        xf = _post_mixer(xf, o, (pf, i), wo, lw("f_norm", i), lw("f_w_in", i), lw("f_w_out", i),
                         lw("e_norm", i), lw("e_w_proj", i), lw("e_w_gate", i), tm_post)
    y = xf.reshape(b, l, D_MODEL)
    return (y, s_stack if step else jnp.stack(states), k_new.reshape(b, l, N_KV, HEAD_DIM), v_new.reshape(b, l, N_KV, HEAD_DIM))


def kernel(x_prompt, x_sample, p_prompt, p_sample, state_gla, cache_k, cache_v, a_norm, a_w_in, a_w_gk_up, a_b_gk,
           a_onorm, a_w_o, kv_norm, w_kv, k_norm, b_norm, b_w_q, b_q_norm, b_w_o, f_norm, f_w_in, f_w_out, e_norm,
           e_w_proj, e_w_gate):
    n_a = a_norm.shape[0]
    depth = f_norm.shape[0]
    main = 2 * GLA_QK + 2 * GLA_V
    a_w_in_p = jnp.concatenate(
        [a_w_in[:, :, :main], jnp.pad(a_w_in[:, :, main:], ((0, 0), (0, 0), (0, GLR_PAD - GLA_LR)))], axis=2)
    w = dict(
        a_norm=a_norm.reshape(n_a, 1, D_MODEL),
        a_w_in=a_w_in_p.astype(BF16),
        a_w_gk_up=jnp.pad(a_w_gk_up, ((0, 0), (0, GLR_PAD - GLA_LR), (0, 0))).astype(BF16),
        a_b_gk=a_b_gk.reshape(n_a, 1, GLA_QK),
        a_onorm=a_onorm.reshape(n_a, 1, GLA_DV),
        a_w_o=a_w_o.astype(BF16),
        kv_norm=kv_norm.reshape(1, 1, D_MODEL),
        w_kv=w_kv.astype(BF16)[None],
        k_norm=k_norm.reshape(1, 1, HEAD_DIM),
        b_norm=b_norm.reshape(-1, 1, D_MODEL),
        b_w_q=b_w_q.astype(BF16),
        b_q_norm=b_q_norm.reshape(-1, 1, HEAD_DIM),
        b_w_o=b_w_o.astype(BF16),
        f_norm=f_norm.reshape(depth, 1, D_MODEL),
        f_w_in=f_w_in.astype(BF16),
        f_w_out=f_w_out.astype(BF16),
        e_norm=e_norm.reshape(depth, 1, D_MODEL),
        e_w_proj=e_w_proj.astype(BF16),
        e_w_gate=e_w_gate.astype(BF16),
    )
    seq = x_prompt.shape[1]
    dec_seq = x_sample.shape[1]
    pos_prompt = jnp.arange(seq, dtype=F32)
    pos_sample = PAST_LEN + jnp.arange(dec_seq, dtype=F32)
    y_p, gs_p, k_p, v_p = _trunk(x_prompt, p_prompt, None, None, pos_prompt, w)
    y_s, gs_s, k_s, v_s = _trunk(x_sample, p_sample, state_gla, (cache_k, cache_v), pos_sample, w)
    keep = min(MAX_WINDOW, seq)
    return (y_p, y_s, gs_p, gs_s, k_p[:, seq - keep:], v_p[:, seq - keep:], k_s, v_s)
```

```python
import functools

import jax
import jax.numpy as jnp
import numpy as np
from jax import lax
from jax.experimental import pallas as pl
from jax.experimental.pallas import tpu as pltpu

D_MODEL = 1024
PAST_LEN = 2048
GLA_HEADS = 4
GLA_DK = 128
GLA_DV = 256
GLA_QK = GLA_HEADS * GLA_DK
GLA_V = GLA_HEADS * GLA_DV
GLA_LR = 16
GLA_TAU = 16.0
GLA_CHUNK = 64
HEAD_DIM = 128
N_KV = 4
Q_PER_KV = 2
N_QH_GROUP = N_KV * Q_PER_KV
DIL_GROUPS = ((128, 1), (512, 4), (2048, 16))
N_GROUPS = len(DIL_GROUPS)
MAX_WINDOW = 2048
ROPE_DIM = HEAD_DIM // 4
ROPE_HALF = ROPE_DIM // 2
ROPE_THETA = 500000.0
FFN_HIDDEN = 2816
PLE_DIM = 256
EPS = 1e-6
NEG_INF = -1e30

LANES = 128
GLR_PAD = LANES
VMEM_LIMIT = 56 * 1024 * 1024

F32 = jnp.float32
BF16 = jnp.bfloat16


def _params(*sem):
    return pltpu.CompilerParams(dimension_semantics=sem, vmem_limit_bytes=VMEM_LIMIT)


def _wspec(lw):
    stack, layer = lw
    index = (layer,) + (0,) * (stack.ndim - 1)
    return pl.BlockSpec((None,) + stack.shape[1:], lambda *_: index, pipeline_mode=pl.Buffered(1))


def _rms(x, g):
    var = jnp.mean(x * x, axis=-1, keepdims=True)
    return x * lax.rsqrt(var + EPS) * g


def _log_sigmoid(x):
    return jnp.minimum(x, 0.0) - jnp.log1p(jnp.exp(-jnp.abs(x)))


def _dot(a, b):
    return jnp.dot(a, b, preferred_element_type=F32)


def _dot_nt(a, b):
    return lax.dot_general(a, b, (((1,), (1,)), ((), ())), preferred_element_type=F32)


def _dot_tn(a, b):
    return lax.dot_general(a, b, (((0,), (0,)), ((), ())), preferred_element_type=F32)


def _proj_a_kernel(x_ref, g_ref, w_ref, qk_ref, v_ref, r_ref, glr_ref):
    h = _rms(x_ref[...], g_ref[...]).astype(BF16)
    qk_ref[...] = _dot(h, w_ref[:, 0:2 * GLA_QK])
    v_ref[...] = _dot(h, w_ref[:, 2 * GLA_QK:2 * GLA_QK + GLA_V]).astype(BF16)
    r_ref[...] = _dot(h, w_ref[:, 2 * GLA_QK + GLA_V:2 * GLA_QK + 2 * GLA_V])
    glr_ref[...] = _dot(h, w_ref[:, 2 * GLA_QK + 2 * GLA_V:]).astype(BF16)


def _proj_a(x, g, w, tm):
    m = x.shape[0]
    row = lambda width: pl.BlockSpec((tm, width), lambda i: (i, 0))
    return pl.pallas_call(
        _proj_a_kernel,
        grid=(m // tm,),
        in_specs=[row(D_MODEL), _wspec(g), _wspec(w)],
        out_specs=[row(2 * GLA_QK), row(GLA_V), row(GLA_V), row(GLR_PAD)],
        out_shape=[jax.ShapeDtypeStruct((m, 2 * GLA_QK), F32),
                   jax.ShapeDtypeStruct((m, GLA_V), BF16),
                   jax.ShapeDtypeStruct((m, GLA_V), F32),
                   jax.ShapeDtypeStruct((m, GLR_PAD), BF16)],
        compiler_params=_params("parallel"),
        name="proj_a",
    )(x, g[0], w[0])


def _split3(x):
    hi = x.astype(BF16)
    r1 = x - hi.astype(F32)
    mid = r1.astype(BF16)
    lo = (r1 - mid.astype(F32)).astype(BF16)
    return hi, mid, lo


def _gla_chunk_terms(glr, wgk, bgk, qk, C):
    T = qk.shape[0]
    log_a = _log_sigmoid(_dot(glr, wgk) + bgk) * (1.0 / GLA_TAU)
    row = lax.broadcasted_iota(jnp.int32, (T, T), 0)
    col = lax.broadcasted_iota(jnp.int32, (T, T), 1)
    same_chunk = (row // C) == (col // C)
    causal = same_chunk & (col <= row)
    tril = jnp.where(causal, 1.0, 0.0).astype(BF16)
    chunk_ones = jnp.where(same_chunk, 1.0, 0.0).astype(BF16)
    parts = jnp.concatenate(_split3(log_a), axis=1)
    fold = lambda y: y[:, :GLA_QK] + y[:, GLA_QK:2 * GLA_QK] + y[:, 2 * GLA_QK:]
    cum = fold(_dot(tril, parts))
    total = fold(_dot(chunk_ones, parts))
    q = qk[:, :GLA_QK] * (GLA_DK ** -0.5)
    k = qk[:, GLA_QK:]
    return q * jnp.exp(cum), (k * jnp.exp(-cum)).astype(BF16), k * jnp.exp(total - cum), total, causal


def _state_decay(total_row):
    decay = jnp.exp(jnp.broadcast_to(total_row, (GLA_DK, GLA_DK)).T)
    return jnp.concatenate([decay] * (GLA_DV // GLA_DK), axis=1)


def _gla_gated(o, onorm, gate):
    return _rms(o, onorm) * (gate * jax.nn.sigmoid(gate))


GLA_SEQ_ROWS = 128
GLA_SEQ_BATCH = 4


def _gla_seq_kernel(qk_ref, v_ref, r_ref, glr_ref, wgk_ref, bgk_ref, on_ref, o_ref, s_ref, *, T, C, nb):
    @pl.when(pl.program_id(1) == 0)
    def _():
        s_ref[...] = jnp.zeros_like(s_ref)

    onorm = on_ref[...]
    for e in range(nb):
        q_in, k_in, k_end, total, causal = _gla_chunk_terms(glr_ref[e], wgk_ref[...], bgk_ref[...], qk_ref[e], C)
        q_in, k_end = q_in.astype(BF16), k_end.astype(BF16)
        for h in range(GLA_HEADS):
            lanes = slice(h * GLA_DK, (h + 1) * GLA_DK)
            v = v_ref[e, :, h * GLA_DV:(h + 1) * GLA_DV]
            att = jnp.where(causal, _dot_nt(q_in[:, lanes], k_in[:, lanes]), 0.0).astype(BF16)
            o_intra = _dot(att, v)
            state = s_ref[e, h]
            o_inter = []
            for c in range(T // C):
                rows = slice(c * C, (c + 1) * C)
                o_inter.append(_dot(q_in[rows, lanes], state.astype(BF16)))
                state = state * _state_decay(total[c * C:c * C + 1, lanes]) + _dot_tn(k_end[rows, lanes], v[rows])
            s_ref[e, h] = state
            o = o_intra + jnp.concatenate(o_inter, axis=0)
            og = _gla_gated(o, onorm, r_ref[e, :, h * GLA_DV:(h + 1) * GLA_DV])
            o_ref[e, :, h * GLA_DV:(h + 1) * GLA_DV] = og.astype(o_ref.dtype)


def _gla_seq(qk, v, r, glr, wgk, bgk, onorm):
    b, l, _ = qk.shape
    T, nb = min(GLA_SEQ_ROWS, l), min(GLA_SEQ_BATCH, b)
    tok = lambda width: pl.BlockSpec((nb, T, width), lambda i, t: (i, t, 0))
    return pl.pallas_call(
        functools.partial(_gla_seq_kernel, T=T, C=GLA_CHUNK, nb=nb),
        grid=(b // nb, l // T),
        in_specs=[tok(2 * GLA_QK), tok(GLA_V), tok(GLA_V), tok(GLR_PAD),
                  _wspec(wgk), _wspec(bgk), _wspec(onorm)],
        out_specs=[tok(GLA_V),
                   pl.BlockSpec((nb, GLA_HEADS, GLA_DK, GLA_DV), lambda i, t: (i, 0, 0, 0))],
        out_shape=[jax.ShapeDtypeStruct((b, l, GLA_V), BF16),
                   jax.ShapeDtypeStruct((b, GLA_HEADS, GLA_DK, GLA_DV), F32)],
        compiler_params=_params("parallel", "arbitrary"),
        name="gla_seq",
    )(qk, v, r, glr, wgk[0], bgk[0], onorm[0])


def _gla_step_kernel(qk_ref, v_ref, r_ref, glr_ref, wgk_ref, bgk_ref, on_ref, s0_ref, *rest, nb, L, first):
    o_ref, s_ref = rest[-2:]
    q_in, k_in, k_end, total, causal = _gla_chunk_terms(glr_ref[...], wgk_ref[...], bgk_ref[...], qk_ref[...], L)
    chunk = lax.broadcasted_iota(jnp.int32, (nb * L, 1), 0) // L
    onorm = on_ref[...]
    new_state = s_ref.at[0] if first else s_ref
    if first:
        s_ref[1:] = jnp.zeros((s_ref.shape[0] - 1,) + s_ref.shape[1:], s_ref.dtype)
    for h in range(GLA_HEADS):
        lanes = slice(h * GLA_DK, (h + 1) * GLA_DK)
        v = v_ref[:, h * GLA_DV:(h + 1) * GLA_DV]
        att = jnp.where(causal, _dot_nt(q_in[:, lanes].astype(BF16), k_in[:, lanes]), 0.0).astype(BF16)
        o = _dot(att, v)
        for j in range(nb):
            mine = chunk == j
            state = s0_ref[j, h]
            o = o + _dot(jnp.where(mine, q_in[:, lanes], 0.0).astype(BF16), state.astype(BF16))
            new_state[j, h] = (state * _state_decay(total[j * L:j * L + 1, lanes])
                               + _dot_tn(jnp.where(mine, k_end[:, lanes], 0.0).astype(BF16), v))
        og = _gla_gated(o, onorm, r_ref[:, h * GLA_DV:(h + 1) * GLA_DV])
        o_ref[:, h * GLA_DV:(h + 1) * GLA_DV] = og.astype(o_ref.dtype)


def _gla_step(qk, v, r, glr, wgk, bgk, onorm, s0, s_new, L, nb):
    m = qk.shape[0]
    b = m // L
    rows = nb * L
    tok = lambda width: pl.BlockSpec((rows, width), lambda i: (i, 0))
    s0_stack, layer = s0
    slab = (nb, GLA_HEADS, GLA_DK, GLA_DV)
    st_in = pl.BlockSpec((None,) + slab, lambda i: (layer, i, 0, 0, 0))
    first = s_new is None
    assert first == (layer == 0)
    in_specs = [tok(2 * GLA_QK), tok(GLA_V), tok(GLA_V), tok(GLR_PAD),
                _wspec(wgk), _wspec(bgk), _wspec(onorm), st_in]
    if first:
        extra_args, aliases = [], {}
        st_out = pl.BlockSpec((s0_stack.shape[0],) + slab, lambda i: (0, i, 0, 0, 0))
    else:
        extra_args, aliases = [s_new], {len(in_specs): 1}
        in_specs.append(pl.BlockSpec(memory_space=pl.ANY))
        st_out = pl.BlockSpec((None,) + slab, lambda i: (layer, i, 0, 0, 0))
    return pl.pallas_call(
        functools.partial(_gla_step_kernel, nb=nb, L=L, first=first),
        grid=(b // nb,),
        in_specs=in_specs,
        out_specs=[tok(GLA_V), st_out],
        out_shape=[jax.ShapeDtypeStruct((m, GLA_V), BF16),
                   jax.ShapeDtypeStruct(s0_stack.shape, F32)],
        input_output_aliases=aliases,
        compiler_params=_params("parallel"),
        name="gla_step",
    )(qk, v, r, glr, wgk[0], bgk[0], onorm[0], s0_stack, *extra_args)


HEADS_PER_DOT = 8


def _head_proj_kernel(x_ref, g_ref, w_ref, hg_ref, rope_ref, *out_refs, n_rot, outs):
    h = _rms(x_ref[...], g_ref[...]).astype(BF16)
    hg = hg_ref[...]
    cos, sin_lo, sin_hi = rope_ref[0], rope_ref[1], rope_ref[2]
    averager = jnp.full((HEAD_DIM, HEAD_DIM), 1.0 / HEAD_DIM, BF16)
    n_heads = w_ref.shape[1] // HEAD_DIM
    for start in range(0, n_heads, HEADS_PER_DOT):
        stop = min(start + HEADS_PER_DOT, n_heads)
        y = _dot(h, w_ref[:, start * HEAD_DIM:stop * HEAD_DIM])
        for head in range(start, stop):
            yh = y[:, (head - start) * HEAD_DIM:(head - start + 1) * HEAD_DIM]
            if head < n_rot:
                ms = _dot((yh * yh).astype(BF16), averager)
                yn = yh * lax.rsqrt(ms + EPS) * hg
                yh = (yn * cos + pltpu.roll(yn, HEAD_DIM - ROPE_HALF, axis=1) * sin_lo
                      + pltpu.roll(yn, ROPE_HALF, axis=1) * sin_hi)
            for ref, (first, count, by_head) in zip(out_refs, outs):
                i = head - first
                if 0 <= i < count:
                    if by_head:
                        ref[pl.ds(i, yh.shape[0], stride=count), :] = yh.astype(ref.dtype)
                    else:
                        ref[:, i * HEAD_DIM:(i + 1) * HEAD_DIM] = yh.astype(ref.dtype)


def _head_proj(x, g, w, hg, rope, n_rot, outs, dtypes, tm):
    m = x.shape[0]
    n_rope_tiles = rope.shape[1] // tm
    row = lambda width: pl.BlockSpec((tm, width), lambda i: (i, 0))
    out_specs, out_shape = [], []
    for (first, count, by_head), dt in zip(outs, dtypes):
        if by_head:
            out_specs.append(pl.BlockSpec((tm * count, HEAD_DIM), lambda i: (i, 0)))
            out_shape.append(jax.ShapeDtypeStruct((m * count, HEAD_DIM), dt))
        else:
            out_specs.append(row(count * HEAD_DIM))
            out_shape.append(jax.ShapeDtypeStruct((m, count * HEAD_DIM), dt))
    return pl.pallas_call(
        functools.partial(_head_proj_kernel, n_rot=n_rot, outs=outs),
        grid=(m // tm,),
        in_specs=[row(D_MODEL), _wspec(g), _wspec(w), _wspec(hg),
                  pl.BlockSpec((3, tm, HEAD_DIM), lambda i: (0, i % n_rope_tiles, 0))],
        out_specs=out_specs,
        out_shape=out_shape,
        compiler_params=_params("parallel"),
        name="head_proj",
    )(x, g[0], w[0], hg[0], rope)


def _rope_tables(pos):
    freqs = jnp.power(ROPE_THETA, -jnp.arange(ROPE_HALF, dtype=F32) * 2.0 / ROPE_DIM)
    ang = pos[:, None] * freqs[None, :]
    cos, sin = jnp.cos(ang), jnp.sin(ang)
    n = pos.shape[0]
    rest = HEAD_DIM - ROPE_DIM
    c = jnp.concatenate([cos, cos, jnp.ones((n, rest), F32)], axis=1)
    s_lo = jnp.concatenate([-sin, jnp.zeros((n, ROPE_HALF + rest), F32)], axis=1)
    s_hi = jnp.concatenate([jnp.zeros((n, ROPE_HALF), F32), sin, jnp.zeros((n, rest), F32)], axis=1)
    return jnp.stack([c, s_lo, s_hi])


BAND = 128


def _attn_seq_kernel(*refs, L):
    n_q = N_GROUPS * Q_PER_KV
    q_refs = refs[:n_q]
    k_ref, v_ref, o_ref, og_ref, lse_ref = refs[n_q:]
    scale = HEAD_DIM ** -0.5
    n2 = Q_PER_KV * BAND
    qi = lax.broadcasted_iota(jnp.int32, (n2, 2 * BAND), 0) % BAND
    kj = lax.broadcasted_iota(jnp.int32, (n2, 2 * BAND), 1)
    band_mask = (kj >= qi) & (kj <= qi + BAND)
    first_mask = (lax.broadcasted_iota(jnp.int32, (n2, BAND), 1)
                  <= lax.broadcasted_iota(jnp.int32, (n2, BAND), 0) % BAND)

    order = sorted(range(N_GROUPS), key=lambda g: -DIL_GROUPS[g][1])
    assert DIL_GROUPS[order[-1]][1] == 1
    slot = {g: i for i, g in enumerate(order[:-1])}
    for g in order:
        win, dil = DIL_GROUPS[g]
        assert win // dil == BAND
        for r in range(dil):
            for blk in range(L // dil // BAND):
                q_rows = pl.ds(r + dil * BAND * blk, BAND, stride=dil)
                q = jnp.concatenate([q_refs[g * Q_PER_KV + j][0, q_rows, :]
                                     for j in range(Q_PER_KV)], axis=0).astype(BF16)
                if blk == 0:
                    k_rows, mask = q_rows, first_mask
                else:
                    k_rows = pl.ds(r + dil * BAND * (blk - 1), 2 * BAND, stride=dil)
                    mask = band_mask
                kb = k_ref[0, k_rows, :].astype(BF16)
                vb = v_ref[0, k_rows, :].astype(BF16)
                s = jnp.where(mask, _dot_nt(q, kb) * scale, NEG_INF)
                for j in range(Q_PER_KV):
                    s_j = s[j * BAND:(j + 1) * BAND]
                    m = jnp.max(s_j, axis=-1, keepdims=True)
                    p = jnp.exp(s_j - m)
                    l = jnp.sum(p, axis=-1, keepdims=True)
                    out = _dot(p.astype(BF16), vb) / l
                    lse = jnp.broadcast_to(m + jnp.log(l), (BAND, LANES))
                    if dil > 1:
                        og_ref[slot[g], j, q_rows, :] = out
                        lse_ref[slot[g], j, q_rows, :] = lse
                    else:
                        rows = slice(blk * BAND, (blk + 1) * BAND)
                        others = [(og_ref[i, j, rows, :], lse_ref[i, j, rows, :]) for i in slot.values()]
                        top = functools.reduce(jnp.maximum, [ls for _, ls in others], lse)
                        w = jnp.exp(lse - top)
                        num, den = w * out, w
                        for o_e, ls_e in others:
                            w = jnp.exp(ls_e - top)
                            num, den = num + w * o_e, den + w
                        o_ref[0, rows, j * HEAD_DIM:(j + 1) * HEAD_DIM] = (num / den).astype(o_ref.dtype)


def _attn_seq(q, k, v):
    b, l, _ = q.shape
    pair = Q_PER_KV * HEAD_DIM
    q_spec = lambda g, j: pl.BlockSpec((1, l, HEAD_DIM), lambda i, h: (i, 0, (g * N_KV + h) * Q_PER_KV + j))
    q_specs = [q_spec(g, j) for g in range(N_GROUPS) for j in range(Q_PER_KV)]
    kv_spec = pl.BlockSpec((1, l, HEAD_DIM), lambda i, h: (i, 0, h))
    stat = pltpu.VMEM((N_GROUPS - 1, Q_PER_KV, l, LANES), F32)
    return pl.pallas_call(
        functools.partial(_attn_seq_kernel, L=l),
        grid=(b, N_KV),
        in_specs=q_specs + [kv_spec, kv_spec],
        out_specs=pl.BlockSpec((1, l, pair), lambda i, h: (i, 0, h)),
        out_shape=jax.ShapeDtypeStruct((b, l, N_QH_GROUP * HEAD_DIM), BF16),
        scratch_shapes=[stat, stat],
        compiler_params=_params("parallel", "arbitrary"),
        name="attn_seq",
    )(*([q] * len(q_specs)), k, v)


def _step_key_rows(dil, t):
    return BAND * (N_KV if dil == 1 else t * N_KV) + BAND


def _step_bias(t):
    assert t * N_KV <= BAND
    slot, ti = np.divmod(np.arange(N_QH_GROUP * t), t)
    kvh = (slot // Q_PER_KV)[:, None]
    ti = ti[:, None]
    new = np.arange(BAND)[None, :]
    new_t, new_h, is_new = new // N_KV, new % N_KV, new < t * N_KV
    parts = []
    for win, dil in DIL_GROUPS:
        rows = np.arange(_step_key_rows(dil, t) - BAND)[None, :]
        if dil == 1:
            ok = (rows % N_KV == kvh) & (rows // N_KV >= ti)
            ok_new = is_new & (new_h == kvh) & (new_t <= ti)
        else:
            ok = rows % (t * N_KV) == ti * N_KV + kvh
            ok_new = is_new & (new_h == kvh) & (new_t == ti)
        parts += [ok, ok_new]
    return jnp.asarray(np.where(np.concatenate(parts, axis=1), 0.0, NEG_INF), F32)


def _attn_step_kernel(*refs, T, nb, sources):
    q_ref, kn_ref, vn_ref = refs[:3]
    bias_ref, o_ref = refs[-2:]
    cached = refs[3:-2]
    scale = HEAD_DIM ** -0.5
    pad = jnp.zeros((BAND - T * N_KV, HEAD_DIM), BF16)
    for bj in range(nb):
        tok = slice(bj * T, (bj + 1) * T)
        new = slice(bj * T * N_KV, (bj + 1) * T * N_KV)
        k_new = jnp.concatenate([kn_ref[new, :].astype(BF16), pad], axis=0)
        v_new = jnp.concatenate([vn_ref[new, :].astype(BF16), pad], axis=0)
        blocks = [(ref[bj].reshape(-1, HEAD_DIM).astype(BF16)) for ref in cached]
        scores, values = [], []
        start = 0
        for g, (pair, first, count) in enumerate(sources):
            keys = jnp.concatenate([blocks[2 * pair][first:first + count], k_new], axis=0)
            values.append(jnp.concatenate([blocks[2 * pair + 1][first:first + count], v_new], axis=0))
            q = jnp.concatenate([q_ref[tok, ((g * N_KV + h) * Q_PER_KV + j) * HEAD_DIM:
                                       ((g * N_KV + h) * Q_PER_KV + j + 1) * HEAD_DIM]
                                 for h in range(N_KV) for j in range(Q_PER_KV)], axis=0)
            n = count + BAND
            scores.append(_dot_nt(q.astype(BF16), keys) * scale + bias_ref[:, start:start + n])
            start += n
        m = functools.reduce(jnp.maximum, [jnp.max(s, axis=-1, keepdims=True) for s in scores])
        probs = [jnp.exp(s - m) for s in scores]
        l = functools.reduce(jnp.add, [jnp.sum(p, axis=-1, keepdims=True) for p in probs])
        acc = functools.reduce(jnp.add, [_dot(p.astype(BF16), v) for p, v in zip(probs, values)])
        out = acc / l
        for slot in range(N_QH_GROUP):
            o_ref[tok, slot * HEAD_DIM:(slot + 1) * HEAD_DIM] = out[slot * T:(slot + 1) * T].astype(o_ref.dtype)


def _attn_step(q, k_new, v_new, cache_k, cache_v, t, nb):
    m = q.shape[0]
    b = m // t
    past = cache_k.shape[1]
    assert all(win // dil == BAND and past >= win for win, dil in DIL_GROUPS)
    assert t <= min(d for _, d in DIL_GROUPS if d > 1) and t <= BAND
    views, specs, sources = [], [], {}
    for g, (win, dil) in enumerate(DIL_GROUPS):
        if dil == 1:
            continue
        shape = (b, past // dil, dil * N_KV, HEAD_DIM)
        assert (past - win) % (dil * BAND) == 0
        blk = (past - win) // (dil * BAND)
        spec = pl.BlockSpec((nb, BAND, t * N_KV, HEAD_DIM), lambda i, blk=blk: (i, blk, 0, 0))
        sources[g] = (len(views) // 2, 0, BAND * t * N_KV)
        if dil == t:
            contiguous = (len(views) // 2, win)
        views += [cache_k.reshape(shape), cache_v.reshape(shape)]
        specs += [spec, spec]
    for g, (win, dil) in enumerate(DIL_GROUPS):
        if dil == 1:
            pair, covered = contiguous
            assert covered >= BAND
            sources[g] = (pair, (covered - BAND) * N_KV, BAND * N_KV)
    tok = lambda width: pl.BlockSpec((nb * t, width), lambda i: (i, 0))
    new = pl.BlockSpec((nb * t * N_KV, HEAD_DIM), lambda i: (i, 0))
    bias = _step_bias(t)
    return pl.pallas_call(
        functools.partial(_attn_step_kernel, T=t, nb=nb, sources=tuple(sources[g] for g in range(N_GROUPS))),
        grid=(b // nb,),
        in_specs=[tok(q.shape[1]), new, new] + specs + [pl.BlockSpec(bias.shape, lambda i: (0, 0))],
        out_specs=tok(N_QH_GROUP * HEAD_DIM),
        out_shape=jax.ShapeDtypeStruct((m, N_QH_GROUP * HEAD_DIM), BF16),
        compiler_params=_params("parallel"),
        name="attn_step",
    )(q, k_new, v_new, *views, bias)


def _post_mixer_kernel(x_ref, o_ref, p_ref, wo_ref, fg_ref, fin_ref, fout_ref, eg_ref, eproj_ref, egate_ref, y_ref):
    x = x_ref[...] + _dot(o_ref[...], wo_ref[...])
    h = _rms(x, fg_ref[...]).astype(BF16)
    a = _dot(h, fin_ref[:, :FFN_HIDDEN])
    b = _dot(h, fin_ref[:, FFN_HIDDEN:])
    x = x + _dot((a * jax.nn.sigmoid(a) * b).astype(BF16), fout_ref[...])
    h = _rms(x, eg_ref[...]).astype(BF16)
    gate = jax.nn.sigmoid(_dot(h, egate_ref[...]))
    y_ref[...] = x + _dot(p_ref[...].astype(BF16), eproj_ref[...]) * gate


def _post_mixer(x, o, p, wo, fg, fin, fout, eg, eproj, egate, tm):
    m = x.shape[0]
    row = lambda width: pl.BlockSpec((tm, width), lambda i: (i, 0))
    p_stack, layer = p
    return pl.pallas_call(
        _post_mixer_kernel,
        grid=(m // tm,),
        in_specs=[row(D_MODEL), row(o.shape[1]), pl.BlockSpec((None, tm, PLE_DIM), lambda i: (layer, i, 0)),
                  _wspec(wo), _wspec(fg), _wspec(fin), _wspec(fout), _wspec(eg), _wspec(eproj), _wspec(egate)],
        out_specs=row(D_MODEL),
        out_shape=jax.ShapeDtypeStruct((m, D_MODEL), F32),
        compiler_params=_params("parallel"),
        name="post_mixer",
    )(x, o, p_stack, wo[0], fg[0], fin[0], fout[0], eg[0], eproj[0], egate[0])


GLA_STEP_ROWS = 16
ATTN_STEP_ROWS = 8
PROJ_ROWS = 512
POST_ROWS = 512


def _trunk(x, p, gla_s0, cache, pos, w):
    b, l, _ = x.shape
    m = b * l
    depth = p.shape[0]
    n_a = depth // 2
    step = gla_s0 is not None
    xf = x.reshape(m, D_MODEL)
    pf = p.reshape(depth, m, PLE_DIM)
    tm = min(PROJ_ROWS, m)
    tm_post = min(POST_ROWS, m)
    rope = _rope_tables(pos)
    if step:
        rope = jnp.tile(rope, (1, tm // l, 1))
    lw = lambda name, layer: (w[name], layer)
    states, s_stack = [], None
    k_new = v_new = None
    for i in range(depth):
        if i < n_a:
            qk, v, r, glr = _proj_a(xf, lw("a_norm", i), lw("a_w_in", i), tm)
            gate_w = (lw("a_w_gk_up", i), lw("a_b_gk", i), lw("a_onorm", i))
            if step:
                o, s_stack = _gla_step(qk, v, r, glr, *gate_w, (gla_s0, i), s_stack, l, GLA_STEP_ROWS // l)
            else:
                o, s = _gla_seq(qk.reshape(b, l, -1), v.reshape(b, l, -1), r.reshape(b, l, -1), glr.reshape(b, l, -1),
                                *gate_w)
                o = o.reshape(m, GLA_V)
                states.append(s)
            wo = lw("a_w_o", i)
        else:
            j = i - n_a
            kv_w = N_KV * HEAD_DIM
            if j == 0:
                by_head = [(0, N_KV, True), (N_KV, N_KV, True)]
                flat = [] if step else [(0, N_KV, False), (N_KV, N_KV, False)]
                kv = _head_proj(xf, lw("kv_norm", 0), lw("w_kv", 0), lw("k_norm", 0), rope, N_KV,
                                by_head + flat, (F32,) * (2 + len(flat)), tm)
                k_new, v_new = kv[:2]
            n_q = N_GROUPS * N_QH_GROUP
            (q,) = _head_proj(xf, lw("b_norm", j), lw("b_w_q", j), lw("b_q_norm", j), rope, n_q,
                              [(0, n_q, False)], (F32,), tm)
            if step:
                o = _attn_step(q, k_new, v_new, *cache, l, ATTN_STEP_ROWS // l)
            else:
                o = _attn_seq(q.reshape(b, l, -1), kv[2].reshape(b, l, kv_w), kv[3].reshape(b, l, kv_w))
                o = o.reshape(m, N_QH_GROUP * HEAD_DIM)
            wo = lw("b_w_o", j)
        xf = _post_mixer(xf, o, (pf, i), wo, lw("f_norm", i), lw("f_w_in", i), lw("f_w_out", i),
                         lw("e_norm", i), lw("e_w_proj", i), lw("e_w_gate", i), tm_post)
    y = xf.reshape(b, l, D_MODEL)
    return (y, s_stack if step else jnp.stack(states), k_new.reshape(b, l, N_KV, HEAD_DIM), v_new.reshape(b, l, N_KV, HEAD_DIM))


def kernel(x_prompt, x_sample, p_prompt, p_sample, state_gla, cache_k, cache_v, a_norm, a_w_in, a_w_gk_up, a_b_gk,
           a_onorm, a_w_o, kv_norm, w_kv, k_norm, b_norm, b_w_q, b_q_norm, b_w_o, f_norm, f_w_in, f_w_out, e_norm,
           e_w_proj, e_w_gate):
    n_a = a_norm.shape[0]
    depth = f_norm.shape[0]
    main = 2 * GLA_QK + 2 * GLA_V
    a_w_in_p = jnp.concatenate(
        [a_w_in[:, :, :main], jnp.pad(a_w_in[:, :, main:], ((0, 0), (0, 0), (0, GLR_PAD - GLA_LR)))], axis=2)
    w = dict(
        a_norm=a_norm.reshape(n_a, 1, D_MODEL),
        a_w_in=a_w_in_p.astype(BF16),
        a_w_gk_up=jnp.pad(a_w_gk_up, ((0, 0), (0, GLR_PAD - GLA_LR), (0, 0))).astype(BF16),
        a_b_gk=a_b_gk.reshape(n_a, 1, GLA_QK),
        a_onorm=a_onorm.reshape(n_a, 1, GLA_DV),
        a_w_o=a_w_o.astype(BF16),
        kv_norm=kv_norm.reshape(1, 1, D_MODEL),
        w_kv=w_kv.astype(BF16)[None],
        k_norm=k_norm.reshape(1, 1, HEAD_DIM),
        b_norm=b_norm.reshape(-1, 1, D_MODEL),
        b_w_q=b_w_q.astype(BF16),
        b_q_norm=b_q_norm.reshape(-1, 1, HEAD_DIM),
        b_w_o=b_w_o.astype(BF16),
        f_norm=f_norm.reshape(depth, 1, D_MODEL),
        f_w_in=f_w_in.astype(BF16),
        f_w_out=f_w_out.astype(BF16),
        e_norm=e_norm.reshape(depth, 1, D_MODEL),
        e_w_proj=e_w_proj.astype(BF16),
        e_w_gate=e_w_gate.astype(BF16),
    )
    seq = x_prompt.shape[1]
    dec_seq = x_sample.shape[1]
    pos_prompt = jnp.arange(seq, dtype=F32)
    pos_sample = PAST_LEN + jnp.arange(dec_seq, dtype=F32)
    y_p, gs_p, k_p, v_p = _trunk(x_prompt, p_prompt, None, None, pos_prompt, w)
    y_s, gs_s, k_s, v_s = _trunk(x_sample, p_sample, state_gla, (cache_k, cache_v), pos_sample, w)
    keep = min(MAX_WINDOW, seq)
    return (y_p, y_s, gs_p, gs_s, k_p[:, seq - keep:], v_p[:, seq - keep:], k_s, v_s)
```

```python
import functools

import jax
import jax.numpy as jnp
import numpy as np
from jax import lax
from jax.experimental import pallas as pl
from jax.experimental.pallas import tpu as pltpu

D_MODEL = 1024
PAST_LEN = 2048
GLA_HEADS = 4
GLA_DK = 128
GLA_DV = 256
GLA_QK = GLA_HEADS * GLA_DK
GLA_V = GLA_HEADS * GLA_DV
GLA_LR = 16
GLA_TAU = 16.0
GLA_CHUNK = 64
HEAD_DIM = 128
N_KV = 4
Q_PER_KV = 2
N_QH_GROUP = N_KV * Q_PER_KV
DIL_GROUPS = ((128, 1), (512, 4), (2048, 16))
N_GROUPS = len(DIL_GROUPS)
MAX_WINDOW = 2048
ROPE_DIM = HEAD_DIM // 4
ROPE_HALF = ROPE_DIM // 2
ROPE_THETA = 500000.0
FFN_HIDDEN = 2816
PLE_DIM = 256
EPS = 1e-6
NEG_INF = -1e30

LANES = 128
GLR_PAD = LANES
VMEM_LIMIT = 56 * 1024 * 1024

F32 = jnp.float32
BF16 = jnp.bfloat16


def _params(*sem):
    return pltpu.CompilerParams(dimension_semantics=sem, vmem_limit_bytes=VMEM_LIMIT)


def _wspec(lw):
    stack, layer = lw
    index = (layer,) + (0,) * (stack.ndim - 1)
    return pl.BlockSpec((None,) + stack.shape[1:], lambda *_: index, pipeline_mode=pl.Buffered(1))


def _rms(x, g):
    var = jnp.mean(x * x, axis=-1, keepdims=True)
    return x * lax.rsqrt(var + EPS) * g


def _log_sigmoid(x):
    return jnp.minimum(x, 0.0) - jnp.log1p(jnp.exp(-jnp.abs(x)))


def _dot(a, b):
    return jnp.dot(a, b, preferred_element_type=F32)


def _dot_nt(a, b):
    return lax.dot_general(a, b, (((1,), (1,)), ((), ())), preferred_element_type=F32)


def _dot_tn(a, b):
    return lax.dot_general(a, b, (((0,), (0,)), ((), ())), preferred_element_type=F32)


def _proj_a_kernel(x_ref, g_ref, w_ref, wlr_ref, qk_ref, v_ref, r_ref, glr_ref):
    h = _rms(x_ref[...], g_ref[...]).astype(BF16)
    qk_ref[...] = _dot(h, w_ref[:, 0:2 * GLA_QK])
    v_ref[...] = _dot(h, w_ref[:, 2 * GLA_QK:2 * GLA_QK + GLA_V]).astype(BF16)
    r_ref[...] = _dot(h, w_ref[:, 2 * GLA_QK + GLA_V:2 * GLA_QK + 2 * GLA_V])
    glr_ref[...] = _dot(h, wlr_ref[...]).astype(BF16)


def _proj_a(x, g, w, wlr, tm):
    m = x.shape[0]
    row = lambda width: pl.BlockSpec((tm, width), lambda i: (i, 0))
    return pl.pallas_call(
        _proj_a_kernel,
        grid=(m // tm,),
        in_specs=[row(D_MODEL), _wspec(g), _wspec(w), _wspec(wlr)],
        out_specs=[row(2 * GLA_QK), row(GLA_V), row(GLA_V), row(GLR_PAD)],
        out_shape=[jax.ShapeDtypeStruct((m, 2 * GLA_QK), F32),
                   jax.ShapeDtypeStruct((m, GLA_V), BF16),
                   jax.ShapeDtypeStruct((m, GLA_V), F32),
                   jax.ShapeDtypeStruct((m, GLR_PAD), BF16)],
        compiler_params=_params("parallel"),
        name="proj_a",
    )(x, g[0], w[0], wlr[0])


def _split3(x):
    hi = x.astype(BF16)
    r1 = x - hi.astype(F32)
    mid = r1.astype(BF16)
    lo = (r1 - mid.astype(F32)).astype(BF16)
    return hi, mid, lo


def _gla_chunk_terms(glr, wgk, bgk, qk, C):
    T = qk.shape[0]
    log_a = _log_sigmoid(_dot(glr, wgk) + bgk) * (1.0 / GLA_TAU)
    row = lax.broadcasted_iota(jnp.int32, (T, T), 0)
    col = lax.broadcasted_iota(jnp.int32, (T, T), 1)
    same_chunk = (row // C) == (col // C)
    causal = same_chunk & (col <= row)
    tril = jnp.where(causal, 1.0, 0.0).astype(BF16)
    chunk_ones = jnp.where(same_chunk, 1.0, 0.0).astype(BF16)
    parts = jnp.concatenate(_split3(log_a), axis=1)
    fold = lambda y: y[:, :GLA_QK] + y[:, GLA_QK:2 * GLA_QK] + y[:, 2 * GLA_QK:]
    cum = fold(_dot(tril, parts))
    total = fold(_dot(chunk_ones, parts))
    q = qk[:, :GLA_QK] * (GLA_DK ** -0.5)
    k = qk[:, GLA_QK:]
    return q * jnp.exp(cum), (k * jnp.exp(-cum)).astype(BF16), k * jnp.exp(total - cum), total, causal


def _state_decay(total_row):
    decay = jnp.exp(jnp.broadcast_to(total_row, (GLA_DK, GLA_DK)).T)
    return jnp.concatenate([decay] * (GLA_DV // GLA_DK), axis=1)


def _gla_gated(o, onorm, gate):
    return _rms(o, onorm) * (gate * jax.nn.sigmoid(gate))


GLA_SEQ_ROWS = 128
GLA_SEQ_BATCH = 4


def _gla_seq_kernel(qk_ref, v_ref, r_ref, glr_ref, wgk_ref, bgk_ref, on_ref, o_ref, s_ref, *, T, C, nb):
    @pl.when(pl.program_id(1) == 0)
    def _():
        s_ref[...] = jnp.zeros_like(s_ref)

    onorm = on_ref[...]
    for e in range(nb):
        q_in, k_in, k_end, total, causal = _gla_chunk_terms(glr_ref[e], wgk_ref[...], bgk_ref[...], qk_ref[e], C)
        q_in, k_end = q_in.astype(BF16), k_end.astype(BF16)
        for h in range(GLA_HEADS):
            lanes = slice(h * GLA_DK, (h + 1) * GLA_DK)
            v = v_ref[e, :, h * GLA_DV:(h + 1) * GLA_DV]
            att = jnp.where(causal, _dot_nt(q_in[:, lanes], k_in[:, lanes]), 0.0).astype(BF16)
            o_intra = _dot(att, v)
            state = s_ref[e, h]
            o_inter = []
            for c in range(T // C):
                rows = slice(c * C, (c + 1) * C)
                o_inter.append(_dot(q_in[rows, lanes], state.astype(BF16)))
                state = state * _state_decay(total[c * C:c * C + 1, lanes]) + _dot_tn(k_end[rows, lanes], v[rows])
            s_ref[e, h] = state
            o = o_intra + jnp.concatenate(o_inter, axis=0)
            og = _gla_gated(o, onorm, r_ref[e, :, h * GLA_DV:(h + 1) * GLA_DV])
            o_ref[e, :, h * GLA_DV:(h + 1) * GLA_DV] = og.astype(o_ref.dtype)


def _gla_seq(qk, v, r, glr, wgk, bgk, onorm):
    b, l, _ = qk.shape
    T, nb = min(GLA_SEQ_ROWS, l), min(GLA_SEQ_BATCH, b)
    tok = lambda width: pl.BlockSpec((nb, T, width), lambda i, t: (i, t, 0))
    return pl.pallas_call(
        functools.partial(_gla_seq_kernel, T=T, C=GLA_CHUNK, nb=nb),
        grid=(b // nb, l // T),
        in_specs=[tok(2 * GLA_QK), tok(GLA_V), tok(GLA_V), tok(GLR_PAD),
                  _wspec(wgk), _wspec(bgk), _wspec(onorm)],
        out_specs=[tok(GLA_V),
                   pl.BlockSpec((nb, GLA_HEADS, GLA_DK, GLA_DV), lambda i, t: (i, 0, 0, 0))],
        out_shape=[jax.ShapeDtypeStruct((b, l, GLA_V), BF16),
                   jax.ShapeDtypeStruct((b, GLA_HEADS, GLA_DK, GLA_DV), F32)],
        compiler_params=_params("parallel", "arbitrary"),
        name="gla_seq",
    )(qk, v, r, glr, wgk[0], bgk[0], onorm[0])


def _gla_step_kernel(qk_ref, v_ref, r_ref, glr_ref, wgk_ref, bgk_ref, on_ref, s0_ref, *rest, nb, L, first):
    o_ref, s_ref = rest[-2:]
    q_in, k_in, k_end, total, causal = _gla_chunk_terms(glr_ref[...], wgk_ref[...], bgk_ref[...], qk_ref[...], L)
    chunk = lax.broadcasted_iota(jnp.int32, (nb * L, 1), 0) // L
    onorm = on_ref[...]
    new_state = s_ref.at[0] if first else s_ref
    if first:
        s_ref[1:] = jnp.zeros((s_ref.shape[0] - 1,) + s_ref.shape[1:], s_ref.dtype)
    for h in range(GLA_HEADS):
        lanes = slice(h * GLA_DK, (h + 1) * GLA_DK)
        v = v_ref[:, h * GLA_DV:(h + 1) * GLA_DV]
        att = jnp.where(causal, _dot_nt(q_in[:, lanes].astype(BF16), k_in[:, lanes]), 0.0).astype(BF16)
        o = _dot(att, v)
        for j in range(nb):
            mine = chunk == j
            state = s0_ref[j, h]
            o = o + _dot(jnp.where(mine, q_in[:, lanes], 0.0).astype(BF16), state.astype(BF16))
            new_state[j, h] = (state * _state_decay(total[j * L:j * L + 1, lanes])
                               + _dot_tn(jnp.where(mine, k_end[:, lanes], 0.0).astype(BF16), v))
        og = _gla_gated(o, onorm, r_ref[:, h * GLA_DV:(h + 1) * GLA_DV])
        o_ref[:, h * GLA_DV:(h + 1) * GLA_DV] = og.astype(o_ref.dtype)


def _gla_step(qk, v, r, glr, wgk, bgk, onorm, s0, s_new, L, nb):
    m = qk.shape[0]
    b = m // L
    rows = nb * L
    tok = lambda width: pl.BlockSpec((rows, width), lambda i: (i, 0))
    s0_stack, layer = s0
    slab = (nb, GLA_HEADS, GLA_DK, GLA_DV)
    st_in = pl.BlockSpec((None,) + slab, lambda i: (layer, i, 0, 0, 0))
    first = s_new is None
    assert first == (layer == 0)
    in_specs = [tok(2 * GLA_QK), tok(GLA_V), tok(GLA_V), tok(GLR_PAD),
                _wspec(wgk), _wspec(bgk), _wspec(onorm), st_in]
    if first:
        extra_args, aliases = [], {}
        st_out = pl.BlockSpec((s0_stack.shape[0],) + slab, lambda i: (0, i, 0, 0, 0))
    else:
        extra_args, aliases = [s_new], {len(in_specs): 1}
        in_specs.append(pl.BlockSpec(memory_space=pl.ANY))
        st_out = pl.BlockSpec((None,) + slab, lambda i: (layer, i, 0, 0, 0))
    return pl.pallas_call(
        functools.partial(_gla_step_kernel, nb=nb, L=L, first=first),
        grid=(b // nb,),
        in_specs=in_specs,
        out_specs=[tok(GLA_V), st_out],
        out_shape=[jax.ShapeDtypeStruct((m, GLA_V), BF16),
                   jax.ShapeDtypeStruct(s0_stack.shape, F32)],
        input_output_aliases=aliases,
        compiler_params=_params("parallel"),
        name="gla_step",
    )(qk, v, r, glr, wgk[0], bgk[0], onorm[0], s0_stack, *extra_args)


HEADS_PER_DOT = 8


def _head_proj_kernel(x_ref, rope_ref, *refs, projs):
    x = x_ref[...]
    xn = x * lax.rsqrt(jnp.mean(x * x, axis=-1, keepdims=True) + EPS)
    cos, sin_lo, sin_hi = rope_ref[0], rope_ref[1], rope_ref[2]
    averager = jnp.full((HEAD_DIM, HEAD_DIM), 1.0 / HEAD_DIM, BF16)
    out_refs = refs[3 * len(projs):]
    for p, (n_rot, outs) in enumerate(projs):
        g_ref, w_ref, hg_ref = refs[3 * p:3 * p + 3]
        mine, out_refs = out_refs[:len(outs)], out_refs[len(outs):]
        h = (xn * g_ref[...]).astype(BF16)
        hg = hg_ref[...]
        n_heads = w_ref.shape[1] // HEAD_DIM
        for start in range(0, n_heads, HEADS_PER_DOT):
            stop = min(start + HEADS_PER_DOT, n_heads)
            y = _dot(h, w_ref[:, start * HEAD_DIM:stop * HEAD_DIM])
            for head in range(start, stop):
                yh = y[:, (head - start) * HEAD_DIM:(head - start + 1) * HEAD_DIM]
                if head < n_rot:
                    ms = _dot((yh * yh).astype(BF16), averager)
                    yn = yh * lax.rsqrt(ms + EPS) * hg
                    yh = (yn * cos + pltpu.roll(yn, HEAD_DIM - ROPE_HALF, axis=1) * sin_lo
                          + pltpu.roll(yn, ROPE_HALF, axis=1) * sin_hi)
                for ref, (first, count, by_head) in zip(mine, outs):
                    i = head - first
                    if 0 <= i < count:
                        if by_head:
                            ref[pl.ds(i, yh.shape[0], stride=count), :] = yh.astype(ref.dtype)
                        else:
                            ref[:, i * HEAD_DIM:(i + 1) * HEAD_DIM] = yh.astype(ref.dtype)


def _head_proj(x, projs, rope, tm):
    m = x.shape[0]
    n_rope_tiles = rope.shape[1] // tm
    row = lambda width: pl.BlockSpec((tm, width), lambda i: (i, 0))
    in_specs = [row(D_MODEL), pl.BlockSpec((3, tm, HEAD_DIM), lambda i: (0, i % n_rope_tiles, 0))]
    args, out_specs, out_shape, static = [x, rope], [], [], []
    for g, w, hg, n_rot, outs, dtypes in projs:
        in_specs += [_wspec(g), _wspec(w), _wspec(hg)]
        args += [g[0], w[0], hg[0]]
        static.append((n_rot, tuple(outs)))
        for (first, count, by_head), dt in zip(outs, dtypes):
            if by_head:
                out_specs.append(pl.BlockSpec((tm * count, HEAD_DIM), lambda i: (i, 0)))
                out_shape.append(jax.ShapeDtypeStruct((m * count, HEAD_DIM), dt))
            else:
                out_specs.append(row(count * HEAD_DIM))
                out_shape.append(jax.ShapeDtypeStruct((m, count * HEAD_DIM), dt))
    flat = pl.pallas_call(
        functools.partial(_head_proj_kernel, projs=tuple(static)),
        grid=(m // tm,),
        in_specs=in_specs,
        out_specs=out_specs,
        out_shape=out_shape,
        compiler_params=_params("parallel"),
        name="head_proj",
    )(*args)
    res, flat = [], list(flat)
    for _, _, _, _, outs, _ in projs:
        res.append(flat[:len(outs)])
        flat = flat[len(outs):]
    return res


def _rope_tables(pos):
    freqs = jnp.power(ROPE_THETA, -jnp.arange(ROPE_HALF, dtype=F32) * 2.0 / ROPE_DIM)
    ang = pos[:, None] * freqs[None, :]
    cos, sin = jnp.cos(ang), jnp.sin(ang)
    n = pos.shape[0]
    rest = HEAD_DIM - ROPE_DIM
    c = jnp.concatenate([cos, cos, jnp.ones((n, rest), F32)], axis=1)
    s_lo = jnp.concatenate([-sin, jnp.zeros((n, ROPE_HALF + rest), F32)], axis=1)
    s_hi = jnp.concatenate([jnp.zeros((n, ROPE_HALF), F32), sin, jnp.zeros((n, rest), F32)], axis=1)
    return jnp.stack([c, s_lo, s_hi])


BAND = 128


def _attn_seq_kernel(*refs, L):
    n_q = N_GROUPS * Q_PER_KV
    q_refs = refs[:n_q]
    k_ref, v_ref, o_ref, og_ref, lse_ref = refs[n_q:]
    scale = HEAD_DIM ** -0.5
    n2 = Q_PER_KV * BAND
    qi = lax.broadcasted_iota(jnp.int32, (n2, 2 * BAND), 0) % BAND
    kj = lax.broadcasted_iota(jnp.int32, (n2, 2 * BAND), 1)
    band_mask = (kj >= qi) & (kj <= qi + BAND)
    first_mask = (lax.broadcasted_iota(jnp.int32, (n2, BAND), 1)
                  <= lax.broadcasted_iota(jnp.int32, (n2, BAND), 0) % BAND)

    order = sorted(range(N_GROUPS), key=lambda g: -DIL_GROUPS[g][1])
    assert DIL_GROUPS[order[-1]][1] == 1
    slot = {g: i for i, g in enumerate(order[:-1])}
    for g in order:
        win, dil = DIL_GROUPS[g]
        assert win // dil == BAND
        for r in range(dil):
            for blk in range(L // dil // BAND):
                q_rows = pl.ds(r + dil * BAND * blk, BAND, stride=dil)
                q = jnp.concatenate([q_refs[g * Q_PER_KV + j][0, q_rows, :]
                                     for j in range(Q_PER_KV)], axis=0).astype(BF16)
                if blk == 0:
                    k_rows, mask = q_rows, first_mask
                else:
                    k_rows = pl.ds(r + dil * BAND * (blk - 1), 2 * BAND, stride=dil)
                    mask = band_mask
                kb = k_ref[0, k_rows, :].astype(BF16)
                vb = v_ref[0, k_rows, :].astype(BF16)
                s = jnp.where(mask, _dot_nt(q, kb) * scale, NEG_INF)
                for j in range(Q_PER_KV):
                    s_j = s[j * BAND:(j + 1) * BAND]
                    m = jnp.max(s_j, axis=-1, keepdims=True)
                    p = jnp.exp(s_j - m)
                    l = jnp.sum(p, axis=-1, keepdims=True)
                    out = _dot(p.astype(BF16), vb) / l
                    lse = jnp.broadcast_to(m + jnp.log(l), (BAND, LANES))
                    if dil > 1:
                        og_ref[slot[g], j, q_rows, :] = out
                        lse_ref[slot[g], j, q_rows, :] = lse
                    else:
                        rows = slice(blk * BAND, (blk + 1) * BAND)
                        others = [(og_ref[i, j, rows, :], lse_ref[i, j, rows, :]) for i in slot.values()]
                        top = functools.reduce(jnp.maximum, [ls for _, ls in others], lse)
                        w = jnp.exp(lse - top)
                        num, den = w * out, w
                        for o_e, ls_e in others:
                            w = jnp.exp(ls_e - top)
                            num, den = num + w * o_e, den + w
                        o_ref[0, rows, j * HEAD_DIM:(j + 1) * HEAD_DIM] = (num / den).astype(o_ref.dtype)


def _attn_seq(q, k, v):
    b, l, _ = q.shape
    pair = Q_PER_KV * HEAD_DIM
    q_spec = lambda g, j: pl.BlockSpec((1, l, HEAD_DIM), lambda i, h: (i, 0, (g * N_KV + h) * Q_PER_KV + j))
    q_specs = [q_spec(g, j) for g in range(N_GROUPS) for j in range(Q_PER_KV)]
    kv_spec = pl.BlockSpec((1, l, HEAD_DIM), lambda i, h: (i, 0, h))
    stat = pltpu.VMEM((N_GROUPS - 1, Q_PER_KV, l, LANES), F32)
    return pl.pallas_call(
        functools.partial(_attn_seq_kernel, L=l),
        grid=(b, N_KV),
        in_specs=q_specs + [kv_spec, kv_spec],
        out_specs=pl.BlockSpec((1, l, pair), lambda i, h: (i, 0, h)),
        out_shape=jax.ShapeDtypeStruct((b, l, N_QH_GROUP * HEAD_DIM), BF16),
        scratch_shapes=[stat, stat],
        compiler_params=_params("parallel", "arbitrary"),
        name="attn_seq",
    )(*([q] * len(q_specs)), k, v)


def _step_key_rows(dil, t):
    return BAND * (N_KV if dil == 1 else t * N_KV) + BAND


def _step_bias(t):
    assert t * N_KV <= BAND
    slot, ti = np.divmod(np.arange(N_QH_GROUP * t), t)
    kvh = (slot // Q_PER_KV)[:, None]
    ti = ti[:, None]
    new = np.arange(BAND)[None, :]
    new_t, new_h, is_new = new // N_KV, new % N_KV, new < t * N_KV
    parts = []
    for win, dil in DIL_GROUPS:
        rows = np.arange(_step_key_rows(dil, t) - BAND)[None, :]
        if dil == 1:
            ok = (rows % N_KV == kvh) & (rows // N_KV >= ti)
            ok_new = is_new & (new_h == kvh) & (new_t <= ti)
        else:
            ok = rows % (t * N_KV) == ti * N_KV + kvh
            ok_new = is_new & (new_h == kvh) & (new_t == ti)
        parts += [ok, ok_new]
    return jnp.asarray(np.where(np.concatenate(parts, axis=1), 0.0, NEG_INF), F32)


def _attn_step_kernel(*refs, T, nb, sources):
    q_ref, kn_ref, vn_ref = refs[:3]
    bias_ref, o_ref = refs[-2:]
    cached = refs[3:-2]
    scale = HEAD_DIM ** -0.5
    pad = jnp.zeros((BAND - T * N_KV, HEAD_DIM), BF16)
    for bj in range(nb):
        tok = slice(bj * T, (bj + 1) * T)
        new = slice(bj * T * N_KV, (bj + 1) * T * N_KV)
        k_new = jnp.concatenate([kn_ref[new, :].astype(BF16), pad], axis=0)
        v_new = jnp.concatenate([vn_ref[new, :].astype(BF16), pad], axis=0)
        blocks = [(ref[bj].reshape(-1, HEAD_DIM).astype(BF16)) for ref in cached]
        scores, values = [], []
        start = 0
        for g, (pair, first, count) in enumerate(sources):
            keys = jnp.concatenate([blocks[2 * pair][first:first + count], k_new], axis=0)
            values.append(jnp.concatenate([blocks[2 * pair + 1][first:first + count], v_new], axis=0))
            q = jnp.concatenate([q_ref[tok, ((g * N_KV + h) * Q_PER_KV + j) * HEAD_DIM:
                                       ((g * N_KV + h) * Q_PER_KV + j + 1) * HEAD_DIM]
                                 for h in range(N_KV) for j in range(Q_PER_KV)], axis=0)
            n = count + BAND
            scores.append(_dot_nt(q.astype(BF16), keys) * scale + bias_ref[:, start:start + n])
            start += n
        m = functools.reduce(jnp.maximum, [jnp.max(s, axis=-1, keepdims=True) for s in scores])
        probs = [jnp.exp(s - m) for s in scores]
        l = functools.reduce(jnp.add, [jnp.sum(p, axis=-1, keepdims=True) for p in probs])
        acc = functools.reduce(jnp.add, [_dot(p.astype(BF16), v) for p, v in zip(probs, values)])
        out = acc / l
        for slot in range(N_QH_GROUP):
            o_ref[tok, slot * HEAD_DIM:(slot + 1) * HEAD_DIM] = out[slot * T:(slot + 1) * T].astype(o_ref.dtype)


def _attn_step(q, k_new, v_new, cache_k, cache_v, t, nb):
    m = q.shape[0]
    b = m // t
    past = cache_k.shape[1]
    assert all(win // dil == BAND and past >= win for win, dil in DIL_GROUPS)
    assert t <= min(d for _, d in DIL_GROUPS if d > 1) and t <= BAND
    views, specs, sources = [], [], {}
    for g, (win, dil) in enumerate(DIL_GROUPS):
        if dil == 1:
            continue
        shape = (b, past // dil, dil * N_KV, HEAD_DIM)
        assert (past - win) % (dil * BAND) == 0
        blk = (past - win) // (dil * BAND)
        spec = pl.BlockSpec((nb, BAND, t * N_KV, HEAD_DIM), lambda i, blk=blk: (i, blk, 0, 0))
        sources[g] = (len(views) // 2, 0, BAND * t * N_KV)
        if dil == t:
            contiguous = (len(views) // 2, win)
        views += [cache_k.reshape(shape), cache_v.reshape(shape)]
        specs += [spec, spec]
    for g, (win, dil) in enumerate(DIL_GROUPS):
        if dil == 1:
            pair, covered = contiguous
            assert covered >= BAND
            sources[g] = (pair, (covered - BAND) * N_KV, BAND * N_KV)
    tok = lambda width: pl.BlockSpec((nb * t, width), lambda i: (i, 0))
    new = pl.BlockSpec((nb * t * N_KV, HEAD_DIM), lambda i: (i, 0))
    bias = _step_bias(t)
    return pl.pallas_call(
        functools.partial(_attn_step_kernel, T=t, nb=nb, sources=tuple(sources[g] for g in range(N_GROUPS))),
        grid=(b // nb,),
        in_specs=[tok(q.shape[1]), new, new] + specs + [pl.BlockSpec(bias.shape, lambda i: (0, 0))],
        out_specs=tok(N_QH_GROUP * HEAD_DIM),
        out_shape=jax.ShapeDtypeStruct((m, N_QH_GROUP * HEAD_DIM), BF16),
        compiler_params=_params("parallel"),
        name="attn_step",
    )(q, k_new, v_new, *views, bias)


def _post_mixer_kernel(x_ref, o_ref, p_ref, wo_ref, fg_ref, fin_ref, fout_ref, eg_ref, eproj_ref, egate_ref, y_ref):
    x = x_ref[...] + _dot(o_ref[...], wo_ref[...])
    h = _rms(x, fg_ref[...]).astype(BF16)
    a = _dot(h, fin_ref[:, :FFN_HIDDEN])
    b = _dot(h, fin_ref[:, FFN_HIDDEN:])
    x = x + _dot((a * jax.nn.sigmoid(a) * b).astype(BF16), fout_ref[...])
    h = _rms(x, eg_ref[...]).astype(BF16)
    gate = jax.nn.sigmoid(_dot(h, egate_ref[...]))
    y_ref[...] = x + _dot(p_ref[...].astype(BF16), eproj_ref[...]) * gate


def _post_mixer(x, o, p, wo, fg, fin, fout, eg, eproj, egate, tm):
    m = x.shape[0]
    row = lambda width: pl.BlockSpec((tm, width), lambda i: (i, 0))
    p_stack, layer = p
    return pl.pallas_call(
        _post_mixer_kernel,
        grid=(m // tm,),
        in_specs=[row(D_MODEL), row(o.shape[1]), pl.BlockSpec((None, tm, PLE_DIM), lambda i: (layer, i, 0)),
                  _wspec(wo), _wspec(fg), _wspec(fin), _wspec(fout), _wspec(eg), _wspec(eproj), _wspec(egate)],
        out_specs=row(D_MODEL),
        out_shape=jax.ShapeDtypeStruct((m, D_MODEL), F32),
        compiler_params=_params("parallel"),
        name="post_mixer",
    )(x, o, p_stack, wo[0], fg[0], fin[0], fout[0], eg[0], eproj[0], egate[0])


GLA_STEP_ROWS = 16
ATTN_STEP_ROWS = 8
PROJ_ROWS = 512
POST_ROWS = 512


def _trunk(x, p, gla_s0, cache, pos, w):
    b, l, _ = x.shape
    m = b * l
    depth = p.shape[0]
    n_a = depth // 2
    step = gla_s0 is not None
    xf = x.reshape(m, D_MODEL)
    pf = p.reshape(depth, m, PLE_DIM)
    tm = min(PROJ_ROWS, m)
    tm_post = min(POST_ROWS, m)
    rope = _rope_tables(pos)
    if step:
        rope = jnp.tile(rope, (1, tm // l, 1))
    lw = lambda name, layer: (w[name], layer)
    states, s_stack = [], None
    k_new = v_new = None
    for i in range(depth):
        if i < n_a:
            qk, v, r, glr = _proj_a(xf, lw("a_norm", i), lw("a_w_in", i), lw("a_w_lr", i), tm)
            gate_w = (lw("a_w_gk_up", i), lw("a_b_gk", i), lw("a_onorm", i))
            if step:
                o, s_stack = _gla_step(qk, v, r, glr, *gate_w, (gla_s0, i), s_stack, l, GLA_STEP_ROWS // l)
            else:
                o, s = _gla_seq(qk.reshape(b, l, -1), v.reshape(b, l, -1), r.reshape(b, l, -1), glr.reshape(b, l, -1),
                                *gate_w)
                o = o.reshape(m, GLA_V)
                states.append(s)
            wo = lw("a_w_o", i)
        else:
            j = i - n_a
            kv_w = N_KV * HEAD_DIM
            n_q = N_GROUPS * N_QH_GROUP
            projs = [(lw("b_norm", j), lw("b_w_q", j), lw("b_q_norm", j), n_q, [(0, n_q, False)], (F32,))]
            if j == 0:
                by_head = [(0, N_KV, True), (N_KV, N_KV, True)]
                flat = [] if step else [(0, N_KV, False), (N_KV, N_KV, False)]
                projs.append((lw("kv_norm", 0), lw("w_kv", 0), lw("k_norm", 0), N_KV,
                              by_head + flat, (F32,) * (2 + len(flat))))
            res = _head_proj(xf, projs, rope, tm)
            (q,) = res[0]
            if j == 0:
                kv = res[1]
                k_new, v_new = kv[:2]
            if step:
                o = _attn_step(q, k_new, v_new, *cache, l, ATTN_STEP_ROWS // l)
            else:
                o = _attn_seq(q.reshape(b, l, -1), kv[2].reshape(b, l, kv_w), kv[3].reshape(b, l, kv_w))
                o = o.reshape(m, N_QH_GROUP * HEAD_DIM)
            wo = lw("b_w_o", j)
        xf = _post_mixer(xf, o, (pf, i), wo, lw("f_norm", i), lw("f_w_in", i), lw("f_w_out", i),
                         lw("e_norm", i), lw("e_w_proj", i), lw("e_w_gate", i), tm_post)
    y = xf.reshape(b, l, D_MODEL)
    return (y, s_stack if step else jnp.stack(states), k_new.reshape(b, l, N_KV, HEAD_DIM), v_new.reshape(b, l, N_KV, HEAD_DIM))


def kernel(x_prompt, x_sample, p_prompt, p_sample, state_gla, cache_k, cache_v, a_norm, a_w_in, a_w_gk_up, a_b_gk,
           a_onorm, a_w_o, kv_norm, w_kv, k_norm, b_norm, b_w_q, b_q_norm, b_w_o, f_norm, f_w_in, f_w_out, e_norm,
           e_w_proj, e_w_gate):
    n_a = a_norm.shape[0]
    depth = f_norm.shape[0]
    main = 2 * GLA_QK + 2 * GLA_V
    lr_pad = GLR_PAD - GLA_LR
    w = dict(
        a_norm=a_norm.reshape(n_a, 1, D_MODEL),
        a_w_in=a_w_in[:, :, :main].astype(BF16),
        a_w_lr=jnp.pad(a_w_in[:, :, main:], ((0, 0), (0, 0), (0, lr_pad))).astype(BF16),
        a_w_gk_up=jnp.pad(a_w_gk_up, ((0, 0), (0, lr_pad), (0, 0))).astype(BF16),
        a_b_gk=a_b_gk.reshape(n_a, 1, GLA_QK),
        a_onorm=a_onorm.reshape(n_a, 1, GLA_DV),
        a_w_o=a_w_o.astype(BF16),
        kv_norm=kv_norm.reshape(1, 1, D_MODEL),
        w_kv=w_kv.astype(BF16)[None],
        k_norm=k_norm.reshape(1, 1, HEAD_DIM),
        b_norm=b_norm.reshape(-1, 1, D_MODEL),
        b_w_q=b_w_q.astype(BF16),
        b_q_norm=b_q_norm.reshape(-1, 1, HEAD_DIM),
        b_w_o=b_w_o.astype(BF16),
        f_norm=f_norm.reshape(depth, 1, D_MODEL),
        f_w_in=f_w_in.astype(BF16),
        f_w_out=f_w_out.astype(BF16),
        e_norm=e_norm.reshape(depth, 1, D_MODEL),
        e_w_proj=e_w_proj.astype(BF16),
        e_w_gate=e_w_gate.astype(BF16),
    )
    seq = x_prompt.shape[1]
    dec_seq = x_sample.shape[1]
    pos_prompt = jnp.arange(seq, dtype=F32)
    pos_sample = PAST_LEN + jnp.arange(dec_seq, dtype=F32)
    y_p, gs_p, k_p, v_p = _trunk(x_prompt, p_prompt, None, None, pos_prompt, w)
    y_s, gs_s, k_s, v_s = _trunk(x_sample, p_sample, state_gla, (cache_k, cache_v), pos_sample, w)
    keep = min(MAX_WINDOW, seq)
    return (y_p, y_s, gs_p, gs_s, k_p[:, seq - keep:], v_p[:, seq - keep:], k_s, v_s)
```

```python
import functools

import jax
import jax.numpy as jnp
import numpy as np
from jax import lax
from jax.experimental import pallas as pl
from jax.experimental.pallas import tpu as pltpu

D_MODEL = 1024
PAST_LEN = 2048
GLA_HEADS = 4
GLA_DK = 128
GLA_DV = 256
GLA_QK = GLA_HEADS * GLA_DK
GLA_V = GLA_HEADS * GLA_DV
GLA_LR = 16
GLA_TAU = 16.0
GLA_CHUNK = 64
HEAD_DIM = 128
N_KV = 4
Q_PER_KV = 2
N_QH_GROUP = N_KV * Q_PER_KV
DIL_GROUPS = ((128, 1), (512, 4), (2048, 16))
N_GROUPS = len(DIL_GROUPS)
MAX_WINDOW = 2048
ROPE_DIM = HEAD_DIM // 4
ROPE_HALF = ROPE_DIM // 2
ROPE_THETA = 500000.0
FFN_HIDDEN = 2816
PLE_DIM = 256
EPS = 1e-6
NEG_INF = -1e30

LANES = 128
GLR_PAD = LANES
VMEM_LIMIT = 56 * 1024 * 1024

F32 = jnp.float32
BF16 = jnp.bfloat16


def _params(*sem):
    return pltpu.CompilerParams(dimension_semantics=sem, vmem_limit_bytes=VMEM_LIMIT)


def _wspec(lw):
    stack, layer = lw
    index = (layer,) + (0,) * (stack.ndim - 1)
    return pl.BlockSpec((None,) + stack.shape[1:], lambda *_: index, pipeline_mode=pl.Buffered(1))


def _rms(x, g):
    var = jnp.mean(x * x, axis=-1, keepdims=True)
    return x * lax.rsqrt(var + EPS) * g


def _log_sigmoid(x):
    return jnp.minimum(x, 0.0) - jnp.log1p(jnp.exp(-jnp.abs(x)))


def _dot(a, b):
    return jnp.dot(a, b, preferred_element_type=F32)


def _dot_nt(a, b):
    return lax.dot_general(a, b, (((1,), (1,)), ((), ())), preferred_element_type=F32)


def _dot_tn(a, b):
    return lax.dot_general(a, b, (((0,), (0,)), ((), ())), preferred_element_type=F32)


def _proj_a_kernel(x_ref, g_ref, w_ref, wlr_ref, qk_ref, v_ref, r_ref, glr_ref):
    h = _rms(x_ref[...], g_ref[...]).astype(BF16)
    qk_ref[...] = _dot(h, w_ref[:, 0:2 * GLA_QK])
    v_ref[...] = _dot(h, w_ref[:, 2 * GLA_QK:2 * GLA_QK + GLA_V]).astype(BF16)
    r_ref[...] = _dot(h, w_ref[:, 2 * GLA_QK + GLA_V:2 * GLA_QK + 2 * GLA_V])
    glr_ref[...] = _dot(h, wlr_ref[...]).astype(BF16)


def _proj_a(x, g, w, wlr, tm):
    m = x.shape[0]
    row = lambda width: pl.BlockSpec((tm, width), lambda i: (i, 0))
    return pl.pallas_call(
        _proj_a_kernel,
        grid=(m // tm,),
        in_specs=[row(D_MODEL), _wspec(g), _wspec(w), _wspec(wlr)],
        out_specs=[row(2 * GLA_QK), row(GLA_V), row(GLA_V), row(GLR_PAD)],
        out_shape=[jax.ShapeDtypeStruct((m, 2 * GLA_QK), F32),
                   jax.ShapeDtypeStruct((m, GLA_V), BF16),
                   jax.ShapeDtypeStruct((m, GLA_V), F32),
                   jax.ShapeDtypeStruct((m, GLR_PAD), BF16)],
        compiler_params=_params("parallel"),
        name="proj_a",
    )(x, g[0], w[0], wlr[0])


def _split3(x):
    hi = x.astype(BF16)
    r1 = x - hi.astype(F32)
    mid = r1.astype(BF16)
    lo = (r1 - mid.astype(F32)).astype(BF16)
    return hi, mid, lo


def _gla_chunk_terms(glr, wgk, bgk, qk, C):
    T = qk.shape[0]
    log_a = _log_sigmoid(_dot(glr, wgk) + bgk) * (1.0 / GLA_TAU)
    row = lax.broadcasted_iota(jnp.int32, (T, T), 0)
    col = lax.broadcasted_iota(jnp.int32, (T, T), 1)
    same_chunk = (row // C) == (col // C)
    causal = same_chunk & (col <= row)
    tril = jnp.where(causal, 1.0, 0.0).astype(BF16)
    chunk_ones = jnp.where(same_chunk, 1.0, 0.0).astype(BF16)
    parts = jnp.concatenate(_split3(log_a), axis=1)
    fold = lambda y: y[:, :GLA_QK] + y[:, GLA_QK:2 * GLA_QK] + y[:, 2 * GLA_QK:]
    cum = fold(_dot(tril, parts))
    total = fold(_dot(chunk_ones, parts))
    q = qk[:, :GLA_QK] * (GLA_DK ** -0.5)
    k = qk[:, GLA_QK:]
    return q * jnp.exp(cum), (k * jnp.exp(-cum)).astype(BF16), k * jnp.exp(total - cum), total, causal


def _state_decay(total_row):
    decay = jnp.exp(jnp.broadcast_to(total_row, (GLA_DK, GLA_DK)).T)
    return jnp.concatenate([decay] * (GLA_DV // GLA_DK), axis=1)


def _gla_gated(o, onorm, gate):
    return _rms(o, onorm) * (gate * jax.nn.sigmoid(gate))


GLA_SEQ_ROWS = 128
GLA_SEQ_BATCH = 4


def _gla_seq_kernel(qk_ref, v_ref, r_ref, glr_ref, wgk_ref, bgk_ref, on_ref, o_ref, s_ref, *, T, C, nb):
    @pl.when(pl.program_id(1) == 0)
    def _():
        s_ref[...] = jnp.zeros_like(s_ref)

    onorm = on_ref[...]
    for e in range(nb):
        q_in, k_in, k_end, total, causal = _gla_chunk_terms(glr_ref[e], wgk_ref[...], bgk_ref[...], qk_ref[e], C)
        q_in, k_end = q_in.astype(BF16), k_end.astype(BF16)
        for h in range(GLA_HEADS):
            lanes = slice(h * GLA_DK, (h + 1) * GLA_DK)
            v = v_ref[e, :, h * GLA_DV:(h + 1) * GLA_DV]
            att = jnp.where(causal, _dot_nt(q_in[:, lanes], k_in[:, lanes]), 0.0).astype(BF16)
            o_intra = _dot(att, v)
            state = s_ref[e, h]
            o_inter = []
            for c in range(T // C):
                rows = slice(c * C, (c + 1) * C)
                o_inter.append(_dot(q_in[rows, lanes], state.astype(BF16)))
                state = state * _state_decay(total[c * C:c * C + 1, lanes]) + _dot_tn(k_end[rows, lanes], v[rows])
            s_ref[e, h] = state
            o = o_intra + jnp.concatenate(o_inter, axis=0)
            og = _gla_gated(o, onorm, r_ref[e, :, h * GLA_DV:(h + 1) * GLA_DV])
            o_ref[e, :, h * GLA_DV:(h + 1) * GLA_DV] = og.astype(o_ref.dtype)


def _gla_seq(qk, v, r, glr, wgk, bgk, onorm):
    b, l, _ = qk.shape
    T, nb = min(GLA_SEQ_ROWS, l), min(GLA_SEQ_BATCH, b)
    tok = lambda width: pl.BlockSpec((nb, T, width), lambda i, t: (i, t, 0))
    return pl.pallas_call(
        functools.partial(_gla_seq_kernel, T=T, C=GLA_CHUNK, nb=nb),
        grid=(b // nb, l // T),
        in_specs=[tok(2 * GLA_QK), tok(GLA_V), tok(GLA_V), tok(GLR_PAD),
                  _wspec(wgk), _wspec(bgk), _wspec(onorm)],
        out_specs=[tok(GLA_V),
                   pl.BlockSpec((nb, GLA_HEADS, GLA_DK, GLA_DV), lambda i, t: (i, 0, 0, 0))],
        out_shape=[jax.ShapeDtypeStruct((b, l, GLA_V), BF16),
                   jax.ShapeDtypeStruct((b, GLA_HEADS, GLA_DK, GLA_DV), F32)],
        compiler_params=_params("parallel", "arbitrary"),
        name="gla_seq",
    )(qk, v, r, glr, wgk[0], bgk[0], onorm[0])


def _gla_step_kernel(qk_ref, v_ref, r_ref, glr_ref, wgk_ref, bgk_ref, on_ref, s0_ref, *rest, nb, L, first):
    o_ref, s_ref = rest[-2:]
    q_in, k_in, k_end, total, causal = _gla_chunk_terms(glr_ref[...], wgk_ref[...], bgk_ref[...], qk_ref[...], L)
    chunk = lax.broadcasted_iota(jnp.int32, (nb * L, 1), 0) // L
    onorm = on_ref[...]
    new_state = s_ref.at[0] if first else s_ref
    if first:
        s_ref[1:] = jnp.zeros((s_ref.shape[0] - 1,) + s_ref.shape[1:], s_ref.dtype)
    for h in range(GLA_HEADS):
        lanes = slice(h * GLA_DK, (h + 1) * GLA_DK)
        v = v_ref[:, h * GLA_DV:(h + 1) * GLA_DV]
        att = jnp.where(causal, _dot_nt(q_in[:, lanes].astype(BF16), k_in[:, lanes]), 0.0).astype(BF16)
        o = _dot(att, v)
        for j in range(nb):
            mine = chunk == j
            state = s0_ref[j, h]
            o = o + _dot(jnp.where(mine, q_in[:, lanes], 0.0).astype(BF16), state.astype(BF16))
            new_state[j, h] = (state * _state_decay(total[j * L:j * L + 1, lanes])
                               + _dot_tn(jnp.where(mine, k_end[:, lanes], 0.0).astype(BF16), v))
        og = _gla_gated(o, onorm, r_ref[:, h * GLA_DV:(h + 1) * GLA_DV])
        o_ref[:, h * GLA_DV:(h + 1) * GLA_DV] = og.astype(o_ref.dtype)


def _gla_step(qk, v, r, glr, wgk, bgk, onorm, s0, s_new, L, nb):
    m = qk.shape[0]
    b = m // L
    rows = nb * L
    tok = lambda width: pl.BlockSpec((rows, width), lambda i: (i, 0))
    s0_stack, layer = s0
    slab = (nb, GLA_HEADS, GLA_DK, GLA_DV)
    st_in = pl.BlockSpec((None,) + slab, lambda i: (layer, i, 0, 0, 0))
    first = s_new is None
    assert first == (layer == 0)
    in_specs = [tok(2 * GLA_QK), tok(GLA_V), tok(GLA_V), tok(GLR_PAD),
                _wspec(wgk), _wspec(bgk), _wspec(onorm), st_in]
    if first:
        extra_args, aliases = [], {}
        st_out = pl.BlockSpec((s0_stack.shape[0],) + slab, lambda i: (0, i, 0, 0, 0))
    else:
        extra_args, aliases = [s_new], {len(in_specs): 1}
        in_specs.append(pl.BlockSpec(memory_space=pl.ANY))
        st_out = pl.BlockSpec((None,) + slab, lambda i: (layer, i, 0, 0, 0))
    return pl.pallas_call(
        functools.partial(_gla_step_kernel, nb=nb, L=L, first=first),
        grid=(b // nb,),
        in_specs=in_specs,
        out_specs=[tok(GLA_V), st_out],
        out_shape=[jax.ShapeDtypeStruct((m, GLA_V), BF16),
                   jax.ShapeDtypeStruct(s0_stack.shape, F32)],
        input_output_aliases=aliases,
        compiler_params=_params("parallel"),
        name="gla_step",
    )(qk, v, r, glr, wgk[0], bgk[0], onorm[0], s0_stack, *extra_args)


HEADS_PER_DOT = 8


def _head_proj_kernel(x_ref, rope_ref, *refs, projs):
    x = x_ref[...]
    xn = x * lax.rsqrt(jnp.mean(x * x, axis=-1, keepdims=True) + EPS)
    cos, sin_lo, sin_hi = rope_ref[0], rope_ref[1], rope_ref[2]
    averager = jnp.full((HEAD_DIM, HEAD_DIM), 1.0 / HEAD_DIM, BF16)
    out_refs = refs[3 * len(projs):]
    for p, (n_rot, outs) in enumerate(projs):
        g_ref, w_ref, hg_ref = refs[3 * p:3 * p + 3]
        mine, out_refs = out_refs[:len(outs)], out_refs[len(outs):]
        h = (xn * g_ref[...]).astype(BF16)
        hg = hg_ref[...]
        n_heads = w_ref.shape[1] // HEAD_DIM
        for start in range(0, n_heads, HEADS_PER_DOT):
            stop = min(start + HEADS_PER_DOT, n_heads)
            y = _dot(h, w_ref[:, start * HEAD_DIM:stop * HEAD_DIM])
            for head in range(start, stop):
                yh = y[:, (head - start) * HEAD_DIM:(head - start + 1) * HEAD_DIM]
                if head < n_rot:
                    ms = _dot((yh * yh).astype(BF16), averager)
                    yn = yh * lax.rsqrt(ms + EPS) * hg
                    yh = (yn * cos + pltpu.roll(yn, HEAD_DIM - ROPE_HALF, axis=1) * sin_lo
                          + pltpu.roll(yn, ROPE_HALF, axis=1) * sin_hi)
                for ref, (first, count, by_head) in zip(mine, outs):
                    i = head - first
                    if 0 <= i < count:
                        if by_head:
                            ref[pl.ds(i, yh.shape[0], stride=count), :] = yh.astype(ref.dtype)
                        else:
                            ref[:, i * HEAD_DIM:(i + 1) * HEAD_DIM] = yh.astype(ref.dtype)


def _head_proj(x, projs, rope, tm):
    m = x.shape[0]
    n_rope_tiles = rope.shape[1] // tm
    row = lambda width: pl.BlockSpec((tm, width), lambda i: (i, 0))
    in_specs = [row(D_MODEL), pl.BlockSpec((3, tm, HEAD_DIM), lambda i: (0, i % n_rope_tiles, 0))]
    args, out_specs, out_shape, static = [x, rope], [], [], []
    for g, w, hg, n_rot, outs, dtypes in projs:
        in_specs += [_wspec(g), _wspec(w), _wspec(hg)]
        args += [g[0], w[0], hg[0]]
        static.append((n_rot, tuple(outs)))
        for (first, count, by_head), dt in zip(outs, dtypes):
            if by_head:
                out_specs.append(pl.BlockSpec((tm * count, HEAD_DIM), lambda i: (i, 0)))
                out_shape.append(jax.ShapeDtypeStruct((m * count, HEAD_DIM), dt))
            else:
                out_specs.append(row(count * HEAD_DIM))
                out_shape.append(jax.ShapeDtypeStruct((m, count * HEAD_DIM), dt))
    flat = pl.pallas_call(
        functools.partial(_head_proj_kernel, projs=tuple(static)),
        grid=(m // tm,),
        in_specs=in_specs,
        out_specs=out_specs,
        out_shape=out_shape,
        compiler_params=_params("parallel"),
        name="head_proj",
    )(*args)
    res, flat = [], list(flat)
    for _, _, _, _, outs, _ in projs:
        res.append(flat[:len(outs)])
        flat = flat[len(outs):]
    return res


def _rope_tables(pos):
    freqs = jnp.power(ROPE_THETA, -jnp.arange(ROPE_HALF, dtype=F32) * 2.0 / ROPE_DIM)
    ang = pos[:, None] * freqs[None, :]
    cos, sin = jnp.cos(ang), jnp.sin(ang)
    n = pos.shape[0]
    rest = HEAD_DIM - ROPE_DIM
    c = jnp.concatenate([cos, cos, jnp.ones((n, rest), F32)], axis=1)
    s_lo = jnp.concatenate([-sin, jnp.zeros((n, ROPE_HALF + rest), F32)], axis=1)
    s_hi = jnp.concatenate([jnp.zeros((n, ROPE_HALF), F32), sin, jnp.zeros((n, rest), F32)], axis=1)
    return jnp.stack([c, s_lo, s_hi])


BAND = 128


def _attn_seq_kernel(*refs, L):
    n_q = N_GROUPS * Q_PER_KV
    q_refs = refs[:n_q]
    k_ref, v_ref, o_ref, og_ref, lse_ref = refs[n_q:]
    scale = HEAD_DIM ** -0.5
    n2 = Q_PER_KV * BAND
    qi = lax.broadcasted_iota(jnp.int32, (n2, 2 * BAND), 0) % BAND
    kj = lax.broadcasted_iota(jnp.int32, (n2, 2 * BAND), 1)
    band_mask = (kj >= qi) & (kj <= qi + BAND)
    first_mask = (lax.broadcasted_iota(jnp.int32, (n2, BAND), 1)
                  <= lax.broadcasted_iota(jnp.int32, (n2, BAND), 0) % BAND)

    order = sorted(range(N_GROUPS), key=lambda g: -DIL_GROUPS[g][1])
    assert DIL_GROUPS[order[-1]][1] == 1
    slot = {g: i for i, g in enumerate(order[:-1])}
    for g in order:
        win, dil = DIL_GROUPS[g]
        assert win // dil == BAND
        for r in range(dil):
            for blk in range(L // dil // BAND):
                q_rows = pl.ds(r + dil * BAND * blk, BAND, stride=dil)
                q = jnp.concatenate([q_refs[g * Q_PER_KV + j][0, q_rows, :]
                                     for j in range(Q_PER_KV)], axis=0).astype(BF16)
                if blk == 0:
                    k_rows, mask = q_rows, first_mask
                else:
                    k_rows = pl.ds(r + dil * BAND * (blk - 1), 2 * BAND, stride=dil)
                    mask = band_mask
                kb = k_ref[0, k_rows, :].astype(BF16)
                vb = v_ref[0, k_rows, :].astype(BF16)
                s = jnp.where(mask, _dot_nt(q, kb) * scale, NEG_INF)
                for j in range(Q_PER_KV):
                    s_j = s[j * BAND:(j + 1) * BAND]
                    m = jnp.max(s_j, axis=-1, keepdims=True)
                    p = jnp.exp(s_j - m)
                    l = jnp.sum(p, axis=-1, keepdims=True)
                    out = _dot(p.astype(BF16), vb) / l
                    lse = jnp.broadcast_to(m + jnp.log(l), (BAND, LANES))
                    if dil > 1:
                        og_ref[slot[g], j, q_rows, :] = out
                        lse_ref[slot[g], j, q_rows, :] = lse
                    else:
                        rows = slice(blk * BAND, (blk + 1) * BAND)
                        others = [(og_ref[i, j, rows, :], lse_ref[i, j, rows, :]) for i in slot.values()]
                        top = functools.reduce(jnp.maximum, [ls for _, ls in others], lse)
                        w = jnp.exp(lse - top)
                        num, den = w * out, w
                        for o_e, ls_e in others:
                            w = jnp.exp(ls_e - top)
                            num, den = num + w * o_e, den + w
                        o_ref[0, rows, j * HEAD_DIM:(j + 1) * HEAD_DIM] = (num / den).astype(o_ref.dtype)


def _attn_seq(q, k, v):
    b, l, _ = q.shape
    pair = Q_PER_KV * HEAD_DIM
    q_spec = lambda g, j: pl.BlockSpec((1, l, HEAD_DIM), lambda i, h: (i, 0, (g * N_KV + h) * Q_PER_KV + j))
    q_specs = [q_spec(g, j) for g in range(N_GROUPS) for j in range(Q_PER_KV)]
    kv_spec = pl.BlockSpec((1, l, HEAD_DIM), lambda i, h: (i, 0, h))
    stat = pltpu.VMEM((N_GROUPS - 1, Q_PER_KV, l, LANES), F32)
    return pl.pallas_call(
        functools.partial(_attn_seq_kernel, L=l),
        grid=(b, N_KV),
        in_specs=q_specs + [kv_spec, kv_spec],
        out_specs=pl.BlockSpec((1, l, pair), lambda i, h: (i, 0, h)),
        out_shape=jax.ShapeDtypeStruct((b, l, N_QH_GROUP * HEAD_DIM), BF16),
        scratch_shapes=[stat, stat],
        compiler_params=_params("parallel", "arbitrary"),
        name="attn_seq",
    )(*([q] * len(q_specs)), k, v)


def _step_key_rows(dil, t):
    return BAND * (N_KV if dil == 1 else t * N_KV) + BAND


def _step_bias(t):
    assert t * N_KV <= BAND
    slot, ti = np.divmod(np.arange(N_QH_GROUP * t), t)
    kvh = (slot // Q_PER_KV)[:, None]
    ti = ti[:, None]
    new = np.arange(BAND)[None, :]
    new_t, new_h, is_new = new // N_KV, new % N_KV, new < t * N_KV
    parts = []
    for win, dil in DIL_GROUPS:
        rows = np.arange(_step_key_rows(dil, t) - BAND)[None, :]
        if dil == 1:
            ok = (rows % N_KV == kvh) & (rows // N_KV >= ti)
            ok_new = is_new & (new_h == kvh) & (new_t <= ti)
        else:
            ok = rows % (t * N_KV) == ti * N_KV + kvh
            ok_new = is_new & (new_h == kvh) & (new_t == ti)
        parts += [ok, ok_new]
    return jnp.asarray(np.where(np.concatenate(parts, axis=1), 0.0, NEG_INF), F32)


def _attn_step_kernel(*refs, T, nb, sources):
    q_ref, kn_ref, vn_ref = refs[:3]
    bias_ref, o_ref = refs[-2:]
    cached = refs[3:-2]
    scale = HEAD_DIM ** -0.5
    pad = jnp.zeros((BAND - T * N_KV, HEAD_DIM), BF16)
    for bj in range(nb):
        tok = slice(bj * T, (bj + 1) * T)
        new = slice(bj * T * N_KV, (bj + 1) * T * N_KV)
        k_new = jnp.concatenate([kn_ref[new, :].astype(BF16), pad], axis=0)
        v_new = jnp.concatenate([vn_ref[new, :].astype(BF16), pad], axis=0)
        blocks = [(ref[bj].reshape(-1, HEAD_DIM).astype(BF16)) for ref in cached]
        scores, values = [], []
        start = 0
        for g, (pair, first, count) in enumerate(sources):
            keys = jnp.concatenate([blocks[2 * pair][first:first + count], k_new], axis=0)
            values.append(jnp.concatenate([blocks[2 * pair + 1][first:first + count], v_new], axis=0))
            q = jnp.concatenate([q_ref[tok, ((g * N_KV + h) * Q_PER_KV + j) * HEAD_DIM:
                                       ((g * N_KV + h) * Q_PER_KV + j + 1) * HEAD_DIM]
                                 for h in range(N_KV) for j in range(Q_PER_KV)], axis=0)
            n = count + BAND
            scores.append(_dot_nt(q.astype(BF16), keys) * scale + bias_ref[:, start:start + n])
            start += n
        m = functools.reduce(jnp.maximum, [jnp.max(s, axis=-1, keepdims=True) for s in scores])
        probs = [jnp.exp(s - m) for s in scores]
        l = functools.reduce(jnp.add, [jnp.sum(p, axis=-1, keepdims=True) for p in probs])
        acc = functools.reduce(jnp.add, [_dot(p.astype(BF16), v) for p, v in zip(probs, values)])
        out = acc / l
        for slot in range(N_QH_GROUP):
            o_ref[tok, slot * HEAD_DIM:(slot + 1) * HEAD_DIM] = out[slot * T:(slot + 1) * T].astype(o_ref.dtype)


def _attn_step(q, k_new, v_new, cache_k, cache_v, t, nb):
    m = q.shape[0]
    b = m // t
    past = cache_k.shape[1]
    assert all(win // dil == BAND and past >= win for win, dil in DIL_GROUPS)
    assert t <= min(d for _, d in DIL_GROUPS if d > 1) and t <= BAND
    views, specs, sources = [], [], {}
    for g, (win, dil) in enumerate(DIL_GROUPS):
        if dil == 1:
            continue
        shape = (b, past // dil, dil * N_KV, HEAD_DIM)
        assert (past - win) % (dil * BAND) == 0
        blk = (past - win) // (dil * BAND)
        spec = pl.BlockSpec((nb, BAND, t * N_KV, HEAD_DIM), lambda i, blk=blk: (i, blk, 0, 0))
        sources[g] = (len(views) // 2, 0, BAND * t * N_KV)
        if dil == t:
            contiguous = (len(views) // 2, win)
        views += [cache_k.reshape(shape), cache_v.reshape(shape)]
        specs += [spec, spec]
    for g, (win, dil) in enumerate(DIL_GROUPS):
        if dil == 1:
            pair, covered = contiguous
            assert covered >= BAND
            sources[g] = (pair, (covered - BAND) * N_KV, BAND * N_KV)
    tok = lambda width: pl.BlockSpec((nb * t, width), lambda i: (i, 0))
    new = pl.BlockSpec((nb * t * N_KV, HEAD_DIM), lambda i: (i, 0))
    bias = _step_bias(t)
    return pl.pallas_call(
        functools.partial(_attn_step_kernel, T=t, nb=nb, sources=tuple(sources[g] for g in range(N_GROUPS))),
        grid=(b // nb,),
        in_specs=[tok(q.shape[1]), new, new] + specs + [pl.BlockSpec(bias.shape, lambda i: (0, 0))],
        out_specs=tok(N_QH_GROUP * HEAD_DIM),
        out_shape=jax.ShapeDtypeStruct((m, N_QH_GROUP * HEAD_DIM), BF16),
        compiler_params=_params("parallel"),
        name="attn_step",
    )(q, k_new, v_new, *views, bias)


def _post_mixer_kernel(x_ref, o_ref, p_ref, wo_ref, fg_ref, fin_ref, fout_ref, eg_ref, eproj_ref, egate_ref, y_ref):
    x = x_ref[...] + _dot(o_ref[...], wo_ref[...])
    h = _rms(x, fg_ref[...]).astype(BF16)
    a = _dot(h, fin_ref[:, :FFN_HIDDEN])
    b = _dot(h, fin_ref[:, FFN_HIDDEN:])
    x = x + _dot((a * jax.nn.sigmoid(a) * b).astype(BF16), fout_ref[...])
    h = _rms(x, eg_ref[...]).astype(BF16)
    gate = jax.nn.sigmoid(_dot(h, egate_ref[...]))
    y_ref[...] = x + _dot(p_ref[...].astype(BF16), eproj_ref[...]) * gate


def _post_mixer(x, o, p, wo, fg, fin, fout, eg, eproj, egate, tm):
    m = x.shape[0]
    row = lambda width: pl.BlockSpec((tm, width), lambda i: (i, 0))
    p_stack, layer = p
    return pl.pallas_call(
        _post_mixer_kernel,
        grid=(m // tm,),
        in_specs=[row(D_MODEL), row(o.shape[1]), pl.BlockSpec((None, tm, PLE_DIM), lambda i: (layer, i, 0)),
                  _wspec(wo), _wspec(fg), _wspec(fin), _wspec(fout), _wspec(eg), _wspec(eproj), _wspec(egate)],
        out_specs=row(D_MODEL),
        out_shape=jax.ShapeDtypeStruct((m, D_MODEL), F32),
        compiler_params=_params("parallel"),
        name="post_mixer",
    )(x, o, p_stack, wo[0], fg[0], fin[0], fout[0], eg[0], eproj[0], egate[0])


GLA_STEP_ROWS = 16
ATTN_STEP_ROWS = 8
PROJ_ROWS = 512
POST_ROWS = 512


def _trunk(x, p, gla_s0, cache, pos, w):
    b, l, _ = x.shape
    m = b * l
    depth = p.shape[0]
    n_a = depth // 2
    step = gla_s0 is not None
    xf = x.reshape(m, D_MODEL)
    pf = p.reshape(depth, m, PLE_DIM)
    tm = min(PROJ_ROWS, m)
    tm_post = min(POST_ROWS, m)
    rope = _rope_tables(pos)
    if step:
        rope = jnp.tile(rope, (1, tm // l, 1))
    lw = lambda name, layer: (w[name], layer)
    states, s_stack = [], None
    k_new = v_new = None
    for i in range(depth):
        if i < n_a:
            qk, v, r, glr = _proj_a(xf, lw("a_norm", i), lw("a_w_in", i), lw("a_w_lr", i), tm)
            gate_w = (lw("a_w_gk_up", i), lw("a_b_gk", i), lw("a_onorm", i))
            if step:
                o, s_stack = _gla_step(qk, v, r, glr, *gate_w, (gla_s0, i), s_stack, l, GLA_STEP_ROWS // l)
            else:
                o, s = _gla_seq(qk.reshape(b, l, -1), v.reshape(b, l, -1), r.reshape(b, l, -1), glr.reshape(b, l, -1),
                                *gate_w)
                o = o.reshape(m, GLA_V)
                states.append(s)
            wo = lw("a_w_o", i)
        else:
            j = i - n_a
            kv_w = N_KV * HEAD_DIM
            n_q = N_GROUPS * N_QH_GROUP
            projs = [(lw("b_norm", j), lw("b_w_q", j), lw("b_q_norm", j), n_q, [(0, n_q, False)], (F32,))]
            if j == 0:
                by_head = [(0, N_KV, True), (N_KV, N_KV, True)]
                flat = [] if step else [(0, N_KV, False), (N_KV, N_KV, False)]
                projs.append((lw("kv_norm", 0), lw("w_kv", 0), lw("k_norm", 0), N_KV,
                              by_head + flat, (F32,) * (2 + len(flat))))
            res = _head_proj(xf, projs, rope, tm)
            (q,) = res[0]
            if j == 0:
                kv = res[1]
                k_new, v_new = kv[:2]
            if step:
                o = _attn_step(q, k_new, v_new, *cache, l, ATTN_STEP_ROWS // l)
            else:
                o = _attn_seq(q.reshape(b, l, -1), kv[2].reshape(b, l, kv_w), kv[3].reshape(b, l, kv_w))
                o = o.reshape(m, N_QH_GROUP * HEAD_DIM)
            wo = lw("b_w_o", j)
        xf = _post_mixer(xf, o, (pf, i), wo, lw("f_norm", i), lw("f_w_in", i), lw("f_w_out", i),
                         lw("e_norm", i), lw("e_w_proj", i), lw("e_w_gate", i), tm_post)
    y = xf.reshape(b, l, D_MODEL)
    return (y, s_stack if step else jnp.stack(states), k_new.reshape(b, l, N_KV, HEAD_DIM), v_new.reshape(b, l, N_KV, HEAD_DIM))


def kernel(x_prompt, x_sample, p_prompt, p_sample, state_gla, cache_k, cache_v, a_norm, a_w_in, a_w_gk_up, a_b_gk,
           a_onorm, a_w_o, kv_norm, w_kv, k_norm, b_norm, b_w_q, b_q_norm, b_w_o, f_norm, f_w_in, f_w_out, e_norm,
           e_w_proj, e_w_gate):
    n_a = a_norm.shape[0]
    depth = f_norm.shape[0]
    main = 2 * GLA_QK + 2 * GLA_V
    lr_pad = GLR_PAD - GLA_LR
    w = dict(
        a_norm=a_norm.reshape(n_a, 1, D_MODEL),
        a_w_in=a_w_in.astype(BF16),
        a_w_lr=jnp.pad(a_w_in[:, :, main:], ((0, 0), (0, 0), (0, lr_pad))).astype(BF16),
        a_w_gk_up=jnp.pad(a_w_gk_up, ((0, 0), (0, lr_pad), (0, 0))).astype(BF16),
        a_b_gk=a_b_gk.reshape(n_a, 1, GLA_QK),
        a_onorm=a_onorm.reshape(n_a, 1, GLA_DV),
        a_w_o=a_w_o.astype(BF16),
        kv_norm=kv_norm.reshape(1, 1, D_MODEL),
        w_kv=w_kv.astype(BF16)[None],
        k_norm=k_norm.reshape(1, 1, HEAD_DIM),
        b_norm=b_norm.reshape(-1, 1, D_MODEL),
        b_w_q=b_w_q.astype(BF16),
        b_q_norm=b_q_norm.reshape(-1, 1, HEAD_DIM),
        b_w_o=b_w_o.astype(BF16),
        f_norm=f_norm.reshape(depth, 1, D_MODEL),
        f_w_in=f_w_in.astype(BF16),
        f_w_out=f_w_out.astype(BF16),
        e_norm=e_norm.reshape(depth, 1, D_MODEL),
        e_w_proj=e_w_proj.astype(BF16),
        e_w_gate=e_w_gate.astype(BF16),
    )
    seq = x_prompt.shape[1]
    dec_seq = x_sample.shape[1]
    pos_prompt = jnp.arange(seq, dtype=F32)
    pos_sample = PAST_LEN + jnp.arange(dec_seq, dtype=F32)
    y_p, gs_p, k_p, v_p = _trunk(x_prompt, p_prompt, None, None, pos_prompt, w)
    y_s, gs_s, k_s, v_s = _trunk(x_sample, p_sample, state_gla, (cache_k, cache_v), pos_sample, w)
    keep = min(MAX_WINDOW, seq)
    return (y_p, y_s, gs_p, gs_s, k_p[:, seq - keep:], v_p[:, seq - keep:], k_s, v_s)
```
